```python
import math
import jax
import jax.numpy as jnp
from jax import lax
import numpy as np

D_MODEL = 1024
BATCH = 8
SEQ = 2048
DEPTH = 4
DEC_BATCH = 128
DEC_SEQ = 8
PAST_LEN = 8192
PAGE_SIZE = 128

N_PAIR = DEPTH // 2
HEAD_DIM = 64
ROT_DIM = HEAD_DIM // 4
ROPE_THETA = 500000.0
A_HEADS = 8
A_KV_HEADS = 2
A_GROUP = A_HEADS // A_KV_HEADS
MOBA_BLOCK = 256
MOBA_TOPK = 3
Q_CHUNK = 128
B_CH = 512
B_CONV_WIDTH = 31
C_HEADS = 8
C_KV_HEADS = 2
C_GROUP = C_HEADS // C_KV_HEADS
WINDOW = 128
D_HEADS = 4
D_DK = 128
D_DV = 128
HGRN_CHUNK = 64
D_FF = 2816
FFN_CONV_WIDTH = 3
LN_EPS = 1e-5
DEEPNORM_ALPHA = (2.0 * DEPTH) ** 0.25
DEEPNORM_BETA = (8.0 * DEPTH) ** -0.25

A_Q = A_HEADS * HEAD_DIM
A_KV = A_KV_HEADS * HEAD_DIM
EVEN_SPLITS = (A_Q, A_Q + A_KV, A_Q + 2 * A_KV, A_Q + 2 * A_KV + B_CH)
EVEN_IN = A_Q + 2 * A_KV + 2 * B_CH
EVEN_MIX = A_Q + B_CH
C_Q = C_HEADS * HEAD_DIM
C_KV = C_KV_HEADS * HEAD_DIM
D_HID = D_HEADS * D_DK
D_VAL = D_HEADS * D_DV
ODD_SPLITS = (C_Q, C_Q + C_KV, C_Q + 2 * C_KV, C_Q + 2 * C_KV + D_HID, C_Q + 2 * C_KV + 2 * D_HID, C_Q + 2 * C_KV + 2 * D_HID + D_VAL)
ODD_IN = C_Q + 2 * C_KV + 2 * D_HID + 2 * D_VAL
ODD_MIX = C_Q + D_VAL

kernel_name = 'hybrid_moba_conformer_swa_hgrn2_step'

F32 = jnp.float32


def layer_norm(x, g, b):
    xf = x.astype(F32)
    mu = xf.mean(-1, keepdims=True)
    var = jnp.square(xf - mu).mean(-1, keepdims=True)
    return ((xf - mu) * lax.rsqrt(var + LN_EPS) * g.astype(F32) + b.astype(F32)).astype(x.dtype)


def rms_norm(x, g):
    xf = x.astype(F32)
    return xf * lax.rsqrt(jnp.mean(xf * xf, -1, keepdims=True) + LN_EPS) * g.astype(F32)


def rope_partial(x, pos):
    half = ROT_DIM // 2
    inv = ROPE_THETA ** (-jnp.arange(half, dtype=F32) * 2.0 / ROT_DIM)
    ang = pos.astype(F32)[:, None] * inv[None, :]
    cos = jnp.cos(ang)[None, :, None, :]
    sin = jnp.sin(ang)[None, :, None, :]
    xr = x[..., :ROT_DIM].astype(F32)
    x1, x2 = xr[..., :half], xr[..., half:]
    rot = jnp.concatenate([x1 * cos - x2 * sin, x2 * cos + x1 * sin], -1).astype(x.dtype)
    return jnp.concatenate([rot, x[..., ROT_DIM:]], -1)


def dwconv(xp, w, b):
    c = w.shape[1]
    y = lax.conv_general_dilated(xp, w[:, None, :].astype(xp.dtype), window_strides=(1,), padding='VALID',
                                 dimension_numbers=('NWC', 'WIO', 'NWC'), feature_group_count=c)
    return y + b.astype(xp.dtype)


def _moba_attend(qc, k_own, v_own, own_mask, sel_k, sel_v):
    bsz, nq = qc.shape[:2]
    scale = HEAD_DIM ** -0.5
    s_own = jnp.einsum('bqkgd,bskd->bqkgs', qc, k_own).astype(F32) * scale
    s_own = jnp.where(own_mask[None, :, None, None, :], s_own, -jnp.inf)
    if sel_k is None:
        p = jax.nn.softmax(s_own, axis=-1).astype(v_own.dtype)
        out = jnp.einsum('bqkgs,bskd->bqkgd', p, v_own)
    else:
        s_sel = jnp.einsum('bqkgd,bqkgnd->bqkgn', qc, sel_k).astype(F32) * scale
        nsel = s_sel.shape[-1]
        p = jax.nn.softmax(jnp.concatenate([s_sel, s_own], -1), axis=-1).astype(v_own.dtype)
        out = (jnp.einsum('bqkgn,bqkgnd->bqkgd', p[..., :nsel], sel_v)
               + jnp.einsum('bqkgs,bskd->bqkgd', p[..., nsel:], v_own))
    return out.reshape(bsz, nq, A_Q)


def moba_prompt(q, k, v):
    bsz, L = q.shape[:2]
    nb_full = L // MOBA_BLOCK
    kfull = k[:, :nb_full * MOBA_BLOCK].reshape(bsz, nb_full, MOBA_BLOCK, A_KV_HEADS, HEAD_DIM)
    vfull = v[:, :nb_full * MOBA_BLOCK].reshape(bsz, nb_full, MOBA_BLOCK, A_KV_HEADS, HEAD_DIM)
    kmean = kfull.astype(F32).mean(axis=2)
    kb = kfull.transpose(0, 3, 1, 2, 4)
    vb = vfull.transpose(0, 3, 1, 2, 4)
    bi = jnp.arange(bsz)[:, None, None, None, None]
    hi = jnp.arange(A_KV_HEADS)[None, None, :, None, None]
    outs = []
    for qs in range(0, L, Q_CHUNK):
        qe = min(qs + Q_CHUNK, L)
        nq = qe - qs
        blk = qs // MOBA_BLOCK
        bs = blk * MOBA_BLOCK
        qc = q[:, qs:qe].reshape(bsz, nq, A_KV_HEADS, A_GROUP, HEAD_DIM)
        own_mask = jnp.arange(bs, qe)[None, :] <= jnp.arange(qs, qe)[:, None]
        sel_k = sel_v = None
        if blk > 0:
            nsel = min(MOBA_TOPK, blk)
            gate = jnp.einsum('bqkgd,bnkd->bqkgn', qc.astype(F32), kmean[:, :blk])
            _, idx = lax.top_k(gate, nsel)
            sel_k = kb[bi, hi, idx].reshape(bsz, nq, A_KV_HEADS, A_GROUP, nsel * MOBA_BLOCK, HEAD_DIM)
            sel_v = vb[bi, hi, idx].reshape(bsz, nq, A_KV_HEADS, A_GROUP, nsel * MOBA_BLOCK, HEAD_DIM)
        outs.append(_moba_attend(qc, k[:, bs:qe], v[:, bs:qe], own_mask, sel_k, sel_v))
    return jnp.concatenate(outs, axis=1)


def moba_sample(q, k_new, v_new, k_pool, v_pool, page_table):
    bsz, S = q.shape[:2]
    ppb = MOBA_BLOCK // PAGE_SIZE
    nb = PAST_LEN // MOBA_BLOCK
    bs = nb * MOBA_BLOCK
    n_own = PAST_LEN - bs
    own_pages = page_table[:, nb * ppb:PAST_LEN // PAGE_SIZE]
    k_own = jnp.concatenate([k_pool[own_pages].reshape(bsz, n_own, A_KV_HEADS, HEAD_DIM).astype(k_new.dtype), k_new], 1)
    v_own = jnp.concatenate([v_pool[own_pages].reshape(bsz, n_own, A_KV_HEADS, HEAD_DIM).astype(v_new.dtype), v_new], 1)
    qc = q.reshape(bsz, S, A_KV_HEADS, A_GROUP, HEAD_DIM)
    qpos = PAST_LEN + jnp.arange(S)
    own_mask = jnp.arange(bs, PAST_LEN + S)[None, :] <= qpos[:, None]
    sel_k = sel_v = None
    if nb > 0:
        nsel = min(MOBA_TOPK, nb)
        kpast = k_pool[page_table[:, :nb * ppb]].reshape(bsz, nb, MOBA_BLOCK, A_KV_HEADS, HEAD_DIM)
        kmean = kpast.astype(F32).mean(axis=2)
        gate = jnp.einsum('bqkgd,bnkd->bqkgn', qc.astype(F32), kmean)
        _, idx = lax.top_k(gate, nsel)
        lpage = idx[..., None] * ppb + jnp.arange(ppb)
        phys = page_table[jnp.arange(bsz)[:, None, None, None, None, None], lpage]
        hi = jnp.arange(A_KV_HEADS)[None, None, :, None, None, None, None]
        rows = jnp.arange(PAGE_SIZE)
        sel_k = k_pool[phys[..., None], rows, hi].reshape(bsz, S, A_KV_HEADS, A_GROUP, nsel * MOBA_BLOCK, HEAD_DIM).astype(k_new.dtype)
        sel_v = v_pool[phys[..., None], rows, hi].reshape(bsz, S, A_KV_HEADS, A_GROUP, nsel * MOBA_BLOCK, HEAD_DIM).astype(v_new.dtype)
    return _moba_attend(qc, k_own, v_own, own_mask, sel_k, sel_v)


def softmax_with_sink(s, sink):
    col = jnp.broadcast_to(sink.reshape(C_KV_HEADS, C_GROUP)[:, :, None, None].astype(F32), s.shape[:-1] + (1,))
    p = jax.nn.softmax(jnp.concatenate([s, col], -1), axis=-1)
    return p[..., :-1]


def swa_prompt(q, k, v, sink):
    bsz, L = q.shape[:2]
    nq = L // WINDOW
    scale = HEAD_DIM ** -0.5
    qb = q.reshape(bsz, nq, WINDOW, C_KV_HEADS, C_GROUP, HEAD_DIM)
    kb = k.reshape(bsz, nq, WINDOW, C_KV_HEADS, HEAD_DIM)
    vb = v.reshape(bsz, nq, WINDOW, C_KV_HEADS, HEAD_DIM)
    kk = jnp.concatenate([jnp.concatenate([jnp.zeros_like(kb[:, :1]), kb[:, :-1]], 1), kb], 2)
    vv = jnp.concatenate([jnp.concatenate([jnp.zeros_like(vb[:, :1]), vb[:, :-1]], 1), vb], 2)
    s = jnp.einsum('bnqkgd,bnskd->bnkgqs', qb, kk).astype(F32) * scale
    blk = jnp.arange(nq)[:, None, None] * WINDOW
    qpos = blk + jnp.arange(WINDOW)[None, :, None]
    kpos = blk - WINDOW + jnp.arange(2 * WINDOW)[None, None, :]
    rel = qpos - kpos
    valid = (rel >= 0) & (rel <= WINDOW) & (kpos >= 0)
    s = jnp.where(valid[None, :, None, None], s, -jnp.inf)
    p = softmax_with_sink(s, sink).astype(v.dtype)
    out = jnp.einsum('bnkgqs,bnskd->bnqkgd', p, vv).reshape(bsz, L, C_Q)
    keep = min(WINDOW, L)
    return out, k[:, L - keep:], v[:, L - keep:]


def swa_sample(q, k_new, v_new, win_k, win_v, sink):
    bsz, S = q.shape[:2]
    wp = win_k.shape[1]
    scale = HEAD_DIM ** -0.5
    kk = jnp.concatenate([win_k.astype(k_new.dtype), k_new], 1)
    vv = jnp.concatenate([win_v.astype(v_new.dtype), v_new], 1)
    qc = q.reshape(bsz, S, C_KV_HEADS, C_GROUP, HEAD_DIM)
    qpos = PAST_LEN + jnp.arange(S)
    kpos = PAST_LEN - wp + jnp.arange(wp + S)
    rel = qpos[:, None] - kpos[None, :]
    valid = (rel >= 0) & (rel <= WINDOW)
    s = jnp.einsum('bqkgd,bskd->bkgqs', qc, kk).astype(F32) * scale
    s = jnp.where(valid[None, None, None], s, -jnp.inf)
    p = softmax_with_sink(s, sink).astype(v_new.dtype)
    out = jnp.einsum('bkgqs,bskd->bqkgd', p, vv).reshape(bsz, S, C_Q)
    keep = min(WINDOW, PAST_LEN + S)
    n = wp + S
    return out, kk[:, n - keep:], vv[:, n - keep:]


def hgrn2_chunked(q, logf, k, v, s0):
    bsz, L = q.shape[:2]
    c = math.gcd(L, HGRN_CHUNK)
    n = L // c

    def to_chunks(t):
        return t.reshape(bsz, n, c, D_HEADS, t.shape[-1]).transpose(1, 0, 3, 2, 4)

    tri = jnp.tril(jnp.ones((c, c), bool))

    def step(S, inp):
        qc, lfc, kc, vc = inp
        G = jnp.cumsum(lfc, axis=2)
        inter = jnp.einsum('bhtd,bhde->bhte', qc * jnp.exp(G), S)
        diff = jnp.where(tri[:, :, None], G[:, :, :, None, :] - G[:, :, None, :, :], -jnp.inf)
        att = jnp.einsum('bhtd,bhsd,bhtsd->bhts', qc, kc, jnp.exp(diff))
        o = inter + jnp.einsum('bhts,bhse->bhte', att, vc)
        gl = G[:, :, -1, :]
        S = jnp.exp(gl)[..., None] * S + jnp.einsum('bhsd,bhse->bhde', kc * jnp.exp(gl[:, :, None, :] - G), vc)
        return S, o

    S, o = lax.scan(step, s0, (to_chunks(q), to_chunks(logf), to_chunks(k), to_chunks(v)))
    o = o.transpose(1, 0, 3, 2, 4).reshape(bsz, L, D_HEADS, D_DV)
    return o, S


def even_mixer(x, pos, prm, i, even_attn, conv_buf):
    bsz, L = x.shape[:2]
    h = x @ prm['w_in_even'][i]
    q, k, v, ga, gb = jnp.split(h, EVEN_SPLITS, axis=-1)
    q = rope_partial(q.reshape(bsz, L, A_HEADS, HEAD_DIM), pos)
    k = rope_partial(k.reshape(bsz, L, A_KV_HEADS, HEAD_DIM), pos)
    v = v.reshape(bsz, L, A_KV_HEADS, HEAD_DIM)
    a_out = even_attn(i, q, k, v)
    u = ga * jax.nn.sigmoid(gb)
    up = jnp.concatenate([conv_buf.astype(u.dtype), u], 1)
    cv = dwconv(up, prm['w_dw_b'][i], prm['b_dw_b'][i])
    cv = jax.nn.silu(layer_norm(cv, prm['conv_ln_g'][i], prm['conv_ln_b'][i]))
    y = jnp.concatenate([a_out.astype(x.dtype), cv.astype(x.dtype)], -1) @ prm['w_out_even'][i]
    return y, k, v, up[:, up.shape[1] - (B_CONV_WIDTH - 1):]


def odd_mixer(x, pos, prm, i, lb, odd_attn, s0):
    bsz, L = x.shape[:2]
    h = x @ prm['w_in_odd'][i]
    q, k, v, dq, df, di, dg = jnp.split(h, ODD_SPLITS, axis=-1)
    q = rope_partial(q.reshape(bsz, L, C_HEADS, HEAD_DIM), pos)
    k = rope_partial(k.reshape(bsz, L, C_KV_HEADS, HEAD_DIM), pos)
    v = v.reshape(bsz, L, C_KV_HEADS, HEAD_DIM)
    c_out, wk, wv = odd_attn(i, q, k, v, prm['sinks'][i])
    hq = jax.nn.silu(dq.astype(F32)).reshape(bsz, L, D_HEADS, D_DK)
    lbh = lb.reshape(D_HEADS, D_DK)
    logf = jnp.logaddexp(jnp.log(lbh), jnp.log1p(-lbh) + jax.nn.log_sigmoid(df.astype(F32).reshape(bsz, L, D_HEADS, D_DK)))
    hk = -jnp.expm1(logf)
    o, s_new = hgrn2_chunked(hq, logf, hk, di.astype(F32).reshape(bsz, L, D_HEADS, D_DV), s0.astype(F32))
    o = rms_norm(o, prm['gnorm_g'][i]) * jax.nn.silu(dg.astype(F32).reshape(bsz, L, D_HEADS, D_DV))
    y = jnp.concatenate([c_out.astype(x.dtype), o.reshape(bsz, L, D_VAL).astype(x.dtype)], -1) @ prm['w_out_odd'][i]
    return y, wk, wv, s_new.astype(s0.dtype)


def conv_ffn(x, prm, l, buf):
    gpre = x @ prm['w_ffn_gate'][l]
    gp = jnp.concatenate([buf.astype(gpre.dtype), gpre], 1)
    hid = jax.nn.gelu(dwconv(gp, prm['w_ffn_conv'][l], prm['b_ffn_conv'][l]), approximate=False) * (x @ prm['w_ffn_up'][l])
    return hid @ prm['w_ffn_down'][l], gp[:, gp.shape[1] - (FFN_CONV_WIDTH - 1):]


def trunk(x, pos, prm, even_attn, odd_attn, conv_b_in, hgrn_in, ffn_in):
    lbs = jnp.cumsum(jax.nn.softmax(prm['lb_param'].astype(F32), axis=0), axis=0)
    lbs = lbs - lbs[:1]
    kv_k, kv_v, conv_b, win_k, win_v, hgrn, ffn = [], [], [], [], [], [], []
    for l in range(DEPTH):
        i = l // 2
        if l % 2 == 0:
            y, kr, vr, cb = even_mixer(x, pos, prm, i, even_attn, conv_b_in[i])
            kv_k.append(kr)
            kv_v.append(vr)
            conv_b.append(cb)
        else:
            y, wk, wv, st = odd_mixer(x, pos, prm, i, lbs[i], odd_attn, hgrn_in[i])
            win_k.append(wk)
            win_v.append(wv)
            hgrn.append(st)
        x = layer_norm(DEEPNORM_ALPHA * x + y, prm['ln1_g'][l], prm['ln1_b'][l])
        f, fb = conv_ffn(x, prm, l, ffn_in[l])
        ffn.append(fb)
        x = layer_norm(DEEPNORM_ALPHA * x + f, prm['ln2_g'][l], prm['ln2_b'][l])
    return (x, jnp.stack(kv_k), jnp.stack(kv_v), jnp.stack(conv_b), jnp.stack(win_k), jnp.stack(win_v),
            jnp.stack(hgrn), jnp.stack(ffn))


def setup_inputs(seed: int = 0) -> dict:
    key = jax.random.key(seed)
    ks = jax.random.split(key, 32)
    n_pages = PAST_LEN // PAGE_SIZE
    n_used = DEC_BATCH * n_pages
    n_pool = n_used + n_used // 4
    w_state = min(WINDOW, PAST_LEN)
    beta = DEEPNORM_BETA

    def nrm(k, shape, scale=1.0):
        return jax.random.normal(k, shape, F32) * scale

    col_e = jnp.ones((EVEN_IN,), F32).at[A_Q + A_KV:A_Q + 2 * A_KV].set(beta)
    col_o = jnp.ones((ODD_IN,), F32).at[C_Q + C_KV:C_Q + 2 * C_KV].set(beta)
    return {
        'x_prompt': nrm(ks[0], (BATCH, SEQ, D_MODEL)),
        'x_sample': nrm(ks[1], (DEC_BATCH, DEC_SEQ, D_MODEL)),
        'cache_k': nrm(ks[2], (N_PAIR, n_pool, PAGE_SIZE, A_KV_HEADS, HEAD_DIM)),
        'cache_v': nrm(ks[3], (N_PAIR, n_pool, PAGE_SIZE, A_KV_HEADS, HEAD_DIM), beta),
        'state_conv_b': nrm(ks[4], (N_PAIR, DEC_BATCH, B_CONV_WIDTH - 1, B_CH), 0.5),
        'state_win_k': nrm(ks[5], (N_PAIR, DEC_BATCH, w_state, C_KV_HEADS, HEAD_DIM)),
        'state_win_v': nrm(ks[6], (N_PAIR, DEC_BATCH, w_state, C_KV_HEADS, HEAD_DIM), beta),
        'state_hgrn': nrm(ks[7], (N_PAIR, DEC_BATCH, D_HEADS, D_DK, D_DV), 0.5),
        'state_ffn': nrm(ks[8], (DEPTH, DEC_BATCH, FFN_CONV_WIDTH - 1, D_FF)),
        'page_table': jax.random.permutation(ks[9], n_pool)[:n_used].reshape(DEC_BATCH, n_pages).astype(jnp.int32),
        'w_in_even': nrm(ks[10], (N_PAIR, D_MODEL, EVEN_IN), D_MODEL ** -0.5) * col_e,
        'w_dw_b': nrm(ks[11], (N_PAIR, B_CONV_WIDTH, B_CH), B_CONV_WIDTH ** -0.5),
        'b_dw_b': nrm(ks[12], (N_PAIR, B_CH), 0.01),
        'conv_ln_g': 1.0 + nrm(ks[13], (N_PAIR, B_CH), 0.02),
        'conv_ln_b': nrm(ks[14], (N_PAIR, B_CH), 0.01),
        'w_out_even': nrm(ks[15], (N_PAIR, EVEN_MIX, D_MODEL), beta * EVEN_MIX ** -0.5),
        'w_in_odd': nrm(ks[16], (N_PAIR, D_MODEL, ODD_IN), D_MODEL ** -0.5) * col_o,
        'sinks': nrm(ks[17], (N_PAIR, C_HEADS), 0.5),
        'lb_param': nrm(ks[18], (N_PAIR, D_HID), 0.5),
        'gnorm_g': 1.0 + nrm(ks[19], (N_PAIR, D_DV), 0.02),
        'w_out_odd': nrm(ks[20], (N_PAIR, ODD_MIX, D_MODEL), beta * ODD_MIX ** -0.5),
        'ln1_g': 1.0 + nrm(ks[21], (DEPTH, D_MODEL), 0.02),
        'ln1_b': nrm(ks[22], (DEPTH, D_MODEL), 0.01),
        'ln2_g': 1.0 + nrm(ks[23], (DEPTH, D_MODEL), 0.02),
        'ln2_b': nrm(ks[24], (DEPTH, D_MODEL), 0.01),
        'w_ffn_gate': nrm(ks[25], (DEPTH, D_MODEL, D_FF), D_MODEL ** -0.5),
        'w_ffn_up': nrm(ks[26], (DEPTH, D_MODEL, D_FF), beta * D_MODEL ** -0.5),
        'w_ffn_conv': nrm(ks[27], (DEPTH, FFN_CONV_WIDTH, D_FF), FFN_CONV_WIDTH ** -0.5),
        'b_ffn_conv': nrm(ks[28], (DEPTH, D_FF), 0.01),
        'w_ffn_down': nrm(ks[29], (DEPTH, D_FF, D_MODEL), beta * D_FF ** -0.5),
    }


def reference(x_prompt, x_sample, cache_k, cache_v, state_conv_b, state_win_k, state_win_v, state_hgrn, state_ffn,
              page_table, w_in_even, w_dw_b, b_dw_b, conv_ln_g, conv_ln_b, w_out_even, w_in_odd, sinks, lb_param,
              gnorm_g, w_out_odd, ln1_g, ln1_b, ln2_g, ln2_b, w_ffn_gate, w_ffn_up, w_ffn_conv, b_ffn_conv, w_ffn_down):
    prm = dict(w_in_even=w_in_even, w_dw_b=w_dw_b, b_dw_b=b_dw_b, conv_ln_g=conv_ln_g, conv_ln_b=conv_ln_b,
               w_out_even=w_out_even, w_in_odd=w_in_odd, sinks=sinks, lb_param=lb_param, gnorm_g=gnorm_g,
               w_out_odd=w_out_odd, ln1_g=ln1_g, ln1_b=ln1_b, ln2_g=ln2_g, ln2_b=ln2_b, w_ffn_gate=w_ffn_gate,
               w_ffn_up=w_ffn_up, w_ffn_conv=w_ffn_conv, b_ffn_conv=b_ffn_conv, w_ffn_down=w_ffn_down)
    bsz = x_prompt.shape[0]
    dt = x_prompt.dtype
    pos_p = jnp.arange(x_prompt.shape[1])
    y_prompt, kv_k_p, kv_v_p, conv_b_p, win_k_p, win_v_p, hgrn_p, ffn_p = trunk(
        x_prompt, pos_p, prm,
        lambda i, q, k, v: moba_prompt(q, k, v),
        lambda i, q, k, v, s: swa_prompt(q, k, v, s),
        jnp.zeros((N_PAIR, bsz, B_CONV_WIDTH - 1, B_CH), dt),
        jnp.zeros((N_PAIR, bsz, D_HEADS, D_DK, D_DV), state_hgrn.dtype),
        jnp.zeros((DEPTH, bsz, FFN_CONV_WIDTH - 1, D_FF), dt))
    pos_s = PAST_LEN + jnp.arange(x_sample.shape[1])
    y_sample, kv_k_s, kv_v_s, conv_b_s, win_k_s, win_v_s, hgrn_s, ffn_s = trunk(
        x_sample, pos_s, prm,
        lambda i, q, k, v: moba_sample(q, k, v, cache_k[i], cache_v[i], page_table),
        lambda i, q, k, v, s: swa_sample(q, k, v, state_win_k[i], state_win_v[i], s),
        state_conv_b, state_hgrn, state_ffn)
    return (y_prompt, y_sample, kv_k_p, kv_v_p, kv_k_s, kv_v_s, conv_b_p, conv_b_s, win_k_p, win_v_p,
            win_k_s, win_v_s, hgrn_p, hgrn_s, ffn_p, ffn_s)
```

```python
import functools

import jax
import jax.numpy as jnp
from jax import lax
from jax.experimental import pallas as pl
from jax.experimental.pallas import tpu as pltpu

F32 = jnp.float32
BF16 = jnp.bfloat16

D_MODEL = 1024
DEPTH = 4
N_PAIR = DEPTH // 2
HEAD_DIM = 64
ROT_DIM = HEAD_DIM // 4
ROPE_THETA = 500000.0
N_HEADS = 8
N_KV = 2
GROUP = N_HEADS // N_KV
QW = N_HEADS * HEAD_DIM
KVW = N_KV * HEAD_DIM
MOBA_BLOCK = 256
MOBA_TOPK = 3
Q_CHUNK = 128
PAGE_SIZE = 128
B_CH = 512
B_CONV_WIDTH = 31
WINDOW = 128
D_HEADS = 4
D_DK = 128
D_FF = 2816
FFN_CONV_WIDTH = 3
LN_EPS = 1e-5
ALPHA = (2.0 * DEPTH) ** 0.25
SCALE = HEAD_DIM ** -0.5
EVEN_IN = QW + 2 * KVW + 2 * B_CH
ODD_IN = QW + 2 * KVW + 4 * 512

LANES = 128
NEG = -1e30
VMEM_LIMIT = 56 * 1024 * 1024

HIGHEST = lax.Precision.HIGHEST


def _nn(a, b, precision=None):
    return lax.dot_general(a, b, (((1,), (0,)), ((), ())), preferred_element_type=F32, precision=precision)


def _nt(a, b, precision=None):
    return lax.dot_general(a, b, (((1,), (1,)), ((), ())), preferred_element_type=F32, precision=precision)


def _ln(z, g, b):
    mu = jnp.mean(z, axis=-1, keepdims=True)
    d = z - mu
    var = jnp.mean(d * d, axis=-1, keepdims=True)
    return d * lax.rsqrt(var + LN_EPS) * g + b


def _silu(x):
    return x * jax.nn.sigmoid(x)


def _params(sem):
    return pltpu.CompilerParams(dimension_semantics=sem, vmem_limit_bytes=VMEM_LIMIT)


def _rope_tables(pos):
    half = ROT_DIM // 2
    inv = ROPE_THETA ** (-jnp.arange(half, dtype=F32) * 2.0 / ROT_DIM)
    ang = pos.astype(F32)[:, None] * inv[None, :]
    cos, sin = jnp.cos(ang), jnp.sin(ang)
    p = pos.shape[0]
    one = jnp.ones((p, HEAD_DIM - ROT_DIM), F32)
    zero = jnp.zeros((p, HEAD_DIM - ROT_DIM), F32)
    zh = jnp.zeros((p, half), F32)
    c = jnp.concatenate([cos, cos, one], 1)
    sa = jnp.concatenate([-sin, zh, zero], 1)
    sb = jnp.concatenate([zh, sin, zero], 1)
    return tuple(jnp.tile(t, (1, LANES // HEAD_DIM)) for t in (c, sa, sb))


def _rope(z, c, sa, sb):
    return z * c + pltpu.roll(z, LANES - ROT_DIM // 2, 1) * sa + pltpu.roll(z, ROT_DIM // 2, 1) * sb


def _inproj_even_body(x_ref, w_ref, c_ref, sa_ref, sb_ref, q_ref, k_ref, v_ref, u_ref):
    xb = x_ref[...].astype(BF16)
    c, sa, sb = c_ref[...], sa_ref[...], sb_ref[...]
    hq = _nn(xb, w_ref[:, 0:QW])
    for s in range(QW // LANES):
        q_ref[:, LANES * s:LANES * (s + 1)] = _rope(hq[:, LANES * s:LANES * (s + 1)], c, sa, sb)
    hkv = _nn(xb, w_ref[:, QW:QW + 2 * KVW])
    k_ref[...] = _rope(hkv[:, 0:KVW], c, sa, sb)
    v_ref[...] = hkv[:, KVW:2 * KVW]
    o = QW + 2 * KVW
    ga = _nn(xb, w_ref[:, o:o + B_CH])
    gb = _nn(xb, w_ref[:, o + B_CH:o + 2 * B_CH])
    u_ref[...] = ga * jax.nn.sigmoid(gb)


def _inproj_odd_body(x_ref, w_ref, c_ref, sa_ref, sb_ref, q_ref, k_ref, v_ref, d_ref):
    xb = x_ref[...].astype(BF16)
    c, sa, sb = c_ref[...], sa_ref[...], sb_ref[...]
    hq = _nn(xb, w_ref[:, 0:QW])
    for s in range(QW // LANES):
        q_ref[:, LANES * s:LANES * (s + 1)] = _rope(hq[:, LANES * s:LANES * (s + 1)], c, sa, sb)
    hkv = _nn(xb, w_ref[:, QW:QW + 2 * KVW])
    k_ref[...] = _rope(hkv[:, 0:KVW], c, sa, sb)
    v_ref[...] = hkv[:, KVW:2 * KVW]
    o = QW + 2 * KVW
    d_ref[:, 0:512] = _silu(_nn(xb, w_ref[:, o:o + 512]))
    d_ref[:, 512:1536] = _nn(xb, w_ref[:, o + 512:o + 1536])
    d_ref[:, 1536:2048] = _silu(_nn(xb, w_ref[:, o + 1536:o + 2048]))


def _inproj(x2, w_bf, tabs, even, tm):
    m = x2.shape[0]
    n_in = w_bf.shape[1]
    ntab = tabs[0].shape[0] // tm
    row = lambda i: (i, 0)
    tab = lambda i: (i % ntab, 0)
    wide = B_CH if even else 2048
    return pl.pallas_call(
        _inproj_even_body if even else _inproj_odd_body,
        grid=(m // tm,),
        in_specs=[pl.BlockSpec((tm, D_MODEL), row), pl.BlockSpec((D_MODEL, n_in), lambda i: (0, 0)),
                  pl.BlockSpec((tm, LANES), tab), pl.BlockSpec((tm, LANES), tab), pl.BlockSpec((tm, LANES), tab)],
        out_specs=[pl.BlockSpec((tm, QW), row), pl.BlockSpec((tm, KVW), row), pl.BlockSpec((tm, KVW), row),
                   pl.BlockSpec((tm, wide), row)],
        out_shape=[jax.ShapeDtypeStruct((m, QW), F32), jax.ShapeDtypeStruct((m, KVW), F32),
                   jax.ShapeDtypeStruct((m, KVW), F32), jax.ShapeDtypeStruct((m, wide), F32)],
        compiler_params=_params(("parallel",)),
        name="inproj_even" if even else "inproj_odd",
    )(x2, w_bf, *tabs)


def _stack_heads(q, j):
    t = q.shape[0]
    lane = lax.broadcasted_iota(jnp.int32, (t, LANES), 1)
    keep = (lane < HEAD_DIM) if j == 0 else (lane >= HEAD_DIM)
    parts = []
    for g in range(GROUP):
        h = GROUP * j + g
        slab = q[:, LANES * (h // 2):LANES * (h // 2 + 1)]
        if h % 2 != j:
            slab = pltpu.roll(slab, HEAD_DIM, 1)
        parts.append(jnp.where(keep, slab, 0.0))
    return jnp.concatenate(parts, 0)


def _unstack_heads(outs, t):
    lane = lax.broadcasted_iota(jnp.int32, (t, LANES), 1)
    slabs = []
    for p in range(N_HEADS // 2):
        halves = []
        for hh in range(2):
            h = 2 * p + hh
            j, g = h // GROUP, h % GROUP
            blk = outs[j][g * t:(g + 1) * t]
            if j != hh:
                blk = pltpu.roll(blk, HEAD_DIM, 1)
            halves.append(blk)
        slabs.append(jnp.where(lane < HEAD_DIM, halves[0], halves[1]))
    return jnp.concatenate(slabs, 1)


def _moba_prompt_body(q_ref, k_ref, v_ref, o_ref, *, nb):
    c = pl.program_id(1)
    blk = c // (MOBA_BLOCK // Q_CHUNK)
    qs0 = c * Q_CHUNK
    rows = GROUP * Q_CHUNK
    q = q_ref[...] * SCALE
    kmean = jnp.concatenate(
        [jnp.mean(k_ref[n * MOBA_BLOCK:(n + 1) * MOBA_BLOCK, :], axis=0, keepdims=True) for n in range(nb)]
        + [jnp.zeros((LANES - nb, LANES), F32)], 0)
    blk_row = lax.broadcasted_iota(jnp.int32, (LANES, rows), 0)
    eye = jnp.where(lax.broadcasted_iota(jnp.int32, (rows, rows), 0) == lax.broadcasted_iota(jnp.int32, (rows, rows), 1),
                    1.0, 0.0).astype(BF16)
    qpos = qs0 + lax.broadcasted_iota(jnp.int32, (rows, MOBA_BLOCK), 0) % Q_CHUNK
    col = lax.broadcasted_iota(jnp.int32, (rows, MOBA_BLOCK), 1)
    erow = lax.broadcasted_iota(jnp.int32, (LANES, MOBA_BLOCK), 0)
    outs = []
    for j in range(N_KV):
        qs = _stack_heads(q, j)
        qsb = qs.astype(BF16)
        gate = jnp.where(blk_row < blk, _nt(kmean, qs, HIGHEST), -jnp.inf)
        sel_rows = []
        for n in range(nb):
            gn = gate[n:n + 1, :]
            beats = jnp.where((gate > gn) | ((gate == gn) & (blk_row < n)), 1.0, 0.0)
            cnt = jnp.sum(beats, axis=0, keepdims=True)
            sel_rows.append(jnp.where((cnt < MOBA_TOPK) & (n < blk), 1.0, 0.0))
        sel_t = jnp.concatenate(sel_rows + [jnp.zeros((LANES - nb, rows), F32)], 0)
        sel = _nt(eye, sel_t.astype(BF16)).astype(BF16)

        def scores(n):
            off = pl.multiple_of(n * MOBA_BLOCK, MOBA_BLOCK)
            kb = k_ref[pl.ds(off, MOBA_BLOCK), :].astype(BF16)
            vb = v_ref[pl.ds(off, MOBA_BLOCK), :].astype(BF16)
            return _nt(qsb, kb), vb

        s, vb = scores(blk)
        s = jnp.where(col + blk * MOBA_BLOCK <= qpos, s, NEG)
        m = jnp.max(s, axis=1, keepdims=True)
        p = jnp.exp(s - m)
        l = jnp.sum(p, axis=1, keepdims=True)
        acc = _nn(p.astype(BF16), vb)

        def past(n, carry):
            m, l, acc = carry
            s, vb = scores(n)
            expand = jnp.where(erow == n, 1.0, 0.0).astype(BF16)
            s = jnp.where(_nn(sel, expand) > 0.5, s, NEG)
            m_new = jnp.maximum(m, jnp.max(s, axis=1, keepdims=True))
            a = jnp.exp(m - m_new)
            p = jnp.exp(s - m_new)
            return m_new, a * l + jnp.sum(p, axis=1, keepdims=True), a * acc + _nn(p.astype(BF16), vb)

        m, l, acc = lax.fori_loop(0, blk, past, (m, l, acc))
        outs.append(acc / l)
    o_ref[...] = _unstack_heads(outs, Q_CHUNK)


def _moba_prompt(q, k, v):
    b, l, _ = q.shape
    nb = l // MOBA_BLOCK
    return pl.pallas_call(
        functools.partial(_moba_prompt_body, nb=nb),
        grid=(b, l // Q_CHUNK),
        in_specs=[pl.BlockSpec((None, Q_CHUNK, QW), lambda i, c: (i, c, 0)),
                  pl.BlockSpec((None, l, KVW), lambda i, c: (i, 0, 0)),
                  pl.BlockSpec((None, l, KVW), lambda i, c: (i, 0, 0))],
        out_specs=pl.BlockSpec((None, Q_CHUNK, QW), lambda i, c: (i, c, 0)),
        out_shape=jax.ShapeDtypeStruct((b, l, QW), F32),
        compiler_params=_params(("parallel", "arbitrary")),
        name="moba_prompt",
    )(q, k, v)


def _stack_decode(q):
    s = q.shape[0]
    lane = lax.broadcasted_iota(jnp.int32, (s, LANES), 1)
    parts = []
    for h in range(N_HEADS):
        j = h // GROUP
        slab = q[:, LANES * (h // 2):LANES * (h // 2 + 1)]
        if h % 2 != j:
            slab = pltpu.roll(slab, HEAD_DIM, 1)
        keep = (lane < HEAD_DIM) if j == 0 else (lane >= HEAD_DIM)
        parts.append(jnp.where(keep, slab, 0.0))
    return jnp.concatenate(parts, 0)


def _unstack_decode(o, s):
    lane = lax.broadcasted_iota(jnp.int32, (s, LANES), 1)
    slabs = []
    for p in range(N_HEADS // 2):
        halves = []
        for hh in range(2):
            h = 2 * p + hh
            blk = o[h * s:(h + 1) * s]
            if h // GROUP != hh:
                blk = pltpu.roll(blk, HEAD_DIM, 1)
            halves.append(blk)
        slabs.append(jnp.where(lane < HEAD_DIM, halves[0], halves[1]))
    return jnp.concatenate(slabs, 1)


def _pad_rows(x, n):
    return jnp.concatenate([x, jnp.zeros((n - x.shape[0], x.shape[1]), x.dtype)], 0)


def _moba_sample_body(pt_ref, q_ref, kn_ref, vn_ref, *rest, nb, s_len):
    ppb = MOBA_BLOCK // PAGE_SIZE
    npages = nb * ppb
    kp, vp = rest[:npages], rest[npages:2 * npages]
    o_ref, s_ref = rest[2 * npages], rest[2 * npages + 1]
    rows = N_HEADS * s_len
    qa = _stack_decode(q_ref[...] * SCALE)
    qab = qa.astype(BF16)
    kmean = jnp.concatenate(
        [sum(jnp.sum(kp[ppb * n + r][...], axis=0, keepdims=True) for r in range(ppb)) * (1.0 / MOBA_BLOCK)
         for n in range(nb)] + [jnp.zeros((LANES - nb, LANES), F32)], 0)
    gate = _nt(qa, kmean, HIGHEST)
    lane = lax.broadcasted_iota(jnp.int32, (rows, LANES), 1)
    gate = jnp.where(lane < nb, gate, -jnp.inf)
    tok = lax.broadcasted_iota(jnp.int32, (rows, LANES), 0) % s_len
    s_new = jnp.where((lane <= tok) & (lane < s_len), _nt(qab, _pad_rows(kn_ref[...], LANES).astype(BF16)), NEG)
    m = jnp.max(s_new, axis=1, keepdims=True)
    for n in range(nb):
        gn = gate[:, n:n + 1]
        beats = jnp.where((gate > gn) | ((gate == gn) & (lane < n)), 1.0, 0.0)
        sel_n = jnp.sum(beats, axis=1, keepdims=True) < MOBA_TOPK
        kb = jnp.concatenate([kp[ppb * n + r][...] for r in range(ppb)], 0).astype(BF16)
        s = jnp.where(sel_n, _nt(qab, kb), NEG)
        s_ref[:, n * MOBA_BLOCK:(n + 1) * MOBA_BLOCK] = s
        m = jnp.maximum(m, jnp.max(s, axis=1, keepdims=True))
    p_new = jnp.exp(s_new - m)
    l = jnp.sum(p_new, axis=1, keepdims=True)
    acc = _nn(p_new.astype(BF16), _pad_rows(vn_ref[...], LANES).astype(BF16))
    for n in range(nb):
        p = jnp.exp(s_ref[:, n * MOBA_BLOCK:(n + 1) * MOBA_BLOCK] - m)
        l = l + jnp.sum(p, axis=1, keepdims=True)
        vb = jnp.concatenate([vp[ppb * n + r][...] for r in range(ppb)], 0).astype(BF16)
        acc = acc + _nn(p.astype(BF16), vb)
    o_ref[...] = _unstack_decode(acc / l, s_len)


def _moba_sample(q, k_new, v_new, k_pool, v_pool, page_table):
    b, s_len, _ = q.shape
    npages = page_table.shape[1]
    assert (npages * PAGE_SIZE) % MOBA_BLOCK == 0 and s_len <= LANES
    nb = npages * PAGE_SIZE // MOBA_BLOCK
    assert 0 < nb <= LANES

    def page_spec(p):
        return pl.BlockSpec((None, PAGE_SIZE, KVW), lambda i, pt: (pt[i, p], 0, 0))

    tok = lambda i, pt: (i, 0, 0)
    grid_spec = pltpu.PrefetchScalarGridSpec(
        num_scalar_prefetch=1,
        grid=(b,),
        in_specs=[pl.BlockSpec((None, s_len, QW), tok), pl.BlockSpec((None, s_len, KVW), tok),
                  pl.BlockSpec((None, s_len, KVW), tok)]
                 + [page_spec(p) for p in range(npages)] + [page_spec(p) for p in range(npages)],
        out_specs=pl.BlockSpec((None, s_len, QW), tok),
        scratch_shapes=[pltpu.VMEM((N_HEADS * s_len, nb * MOBA_BLOCK), F32)],
    )
    return pl.pallas_call(
        functools.partial(_moba_sample_body, nb=nb, s_len=s_len),
        grid_spec=grid_spec,
        out_shape=jax.ShapeDtypeStruct((b, s_len, QW), F32),
        compiler_params=_params(("arbitrary",)),
        name="moba_sample",
    )(page_table, q, k_new, v_new, *([k_pool] * npages), *([v_pool] * npages))


def _swa_prompt_body(q_ref, kp_ref, kc_ref, vp_ref, vc_ref, sink_ref, o_ref):
    t = pl.program_id(1)
    rows = GROUP * WINDOW
    q = q_ref[...] * SCALE
    kk = jnp.concatenate([kp_ref[...], kc_ref[...]], 0).astype(BF16)
    vv = jnp.concatenate([vp_ref[...], vc_ref[...]], 0).astype(BF16)
    tq = lax.broadcasted_iota(jnp.int32, (rows, 2 * WINDOW), 0) % WINDOW
    ck = lax.broadcasted_iota(jnp.int32, (rows, 2 * WINDOW), 1)
    lo = jnp.where(t == 0, WINDOW, 0)
    valid = (ck >= tq) & (ck <= tq + WINDOW) & (ck >= lo)
    outs = []
    for j in range(N_KV):
        s = jnp.where(valid, _nt(_stack_heads(q, j).astype(BF16), kk), NEG)
        sink = sink_ref[j]
        m = jnp.maximum(jnp.max(s, axis=1, keepdims=True), sink)
        p = jnp.exp(s - m)
        den = jnp.sum(p, axis=1, keepdims=True) + jnp.exp(sink - m)
        outs.append(_nn(p.astype(BF16), vv) / den)
    o_ref[...] = _unstack_heads(outs, WINDOW)


def _swa_prompt(q, k, v, sink):
    b, l, _ = q.shape
    sink_rows = jnp.repeat(sink.reshape(N_KV, GROUP), WINDOW, axis=1)[:, :, None].astype(F32)
    cur = lambda i, t: (i, t, 0)
    prev = lambda i, t: (i, jnp.maximum(t - 1, 0), 0)
    return pl.pallas_call(
        _swa_prompt_body,
        grid=(b, l // WINDOW),
        in_specs=[pl.BlockSpec((None, WINDOW, QW), cur),
                  pl.BlockSpec((None, WINDOW, KVW), prev), pl.BlockSpec((None, WINDOW, KVW), cur),
                  pl.BlockSpec((None, WINDOW, KVW), prev), pl.BlockSpec((None, WINDOW, KVW), cur),
                  pl.BlockSpec((N_KV, GROUP * WINDOW, 1), lambda i, t: (0, 0, 0))],
        out_specs=pl.BlockSpec((None, WINDOW, QW), cur),
        out_shape=jax.ShapeDtypeStruct((b, l, QW), F32),
        compiler_params=_params(("parallel", "arbitrary")),
        name="swa_prompt",
    )(q, k, k, v, v, sink_rows)


def _swa_sample_body(q_ref, kn_ref, vn_ref, wk_ref, wv_ref, sink_ref, o_ref, *, bt, s_len):
    rows = N_HEADS * s_len
    tok = lax.broadcasted_iota(jnp.int32, (rows, LANES), 0) % s_len
    lane = lax.broadcasted_iota(jnp.int32, (rows, LANES), 1)
    sink = sink_ref[...]

    def one(i, carry):
        qab = _stack_decode(q_ref[i] * SCALE).astype(BF16)
        s_win = jnp.where(lane >= tok, _nt(qab, wk_ref[i].astype(BF16)), NEG)
        s_new = jnp.where((lane <= tok) & (lane < s_len), _nt(qab, _pad_rows(kn_ref[i], LANES).astype(BF16)), NEG)
        m = jnp.maximum(jnp.maximum(jnp.max(s_win, axis=1, keepdims=True), jnp.max(s_new, axis=1, keepdims=True)), sink)
        p_win = jnp.exp(s_win - m)
        p_new = jnp.exp(s_new - m)
        den = jnp.sum(p_win, axis=1, keepdims=True) + jnp.sum(p_new, axis=1, keepdims=True) + jnp.exp(sink - m)
        acc = _nn(p_win.astype(BF16), wv_ref[i].astype(BF16)) + _nn(p_new.astype(BF16), _pad_rows(vn_ref[i], LANES).astype(BF16))
        o_ref[i] = _unstack_decode(acc / den, s_len)
        return carry

    lax.fori_loop(0, bt, one, 0)


def _swa_sample(q, k_new, v_new, win_k, win_v, sink, bt=16):
    b, s_len, _ = q.shape
    bt = min(bt, b)
    assert win_k.shape[1] == WINDOW == LANES and b % bt == 0
    sink_rows = jnp.repeat(sink.astype(F32), s_len)[:, None]
    blk = lambda i: (i, 0, 0)
    return pl.pallas_call(
        functools.partial(_swa_sample_body, bt=bt, s_len=s_len),
        grid=(b // bt,),
        in_specs=[pl.BlockSpec((bt, s_len, QW), blk), pl.BlockSpec((bt, s_len, KVW), blk),
                  pl.BlockSpec((bt, s_len, KVW), blk), pl.BlockSpec((bt, WINDOW, KVW), blk),
                  pl.BlockSpec((bt, WINDOW, KVW), blk), pl.BlockSpec((N_HEADS * s_len, 1), lambda i: (0, 0))],
        out_specs=pl.BlockSpec((bt, s_len, QW), blk),
        out_shape=jax.ShapeDtypeStruct((b, s_len, QW), F32),
        compiler_params=_params(("parallel",)),
        name="swa_sample",
    )(q, k_new, v_new, win_k, win_v, sink_rows)


def _conv_prompt_body(prev_ref, cur_ref, w_ref, b_ref, g_ref, bb_ref, o_ref, ext_ref, *, tm, halo):
    t = pl.program_id(1)
    ext_ref[0:halo, :] = jnp.where(t == 0, 0.0, prev_ref[...])
    ext_ref[halo:halo + tm, :] = cur_ref[...]
    kw = w_ref.shape[0]
    acc = jnp.zeros((tm, B_CH), F32) + b_ref[...]
    for j in range(kw):
        acc = acc + w_ref[j:j + 1, :] * ext_ref[pl.ds(halo - (kw - 1) + j, tm), :]
    o_ref[...] = _silu(_ln(acc, g_ref[...], bb_ref[...]))


def _conv_prompt(u, w, b, g, bb, tm=256, halo=32):
    bsz, l, _ = u.shape
    r = tm // halo
    vec = lambda i, t: (0, 0)
    return pl.pallas_call(
        functools.partial(_conv_prompt_body, tm=tm, halo=halo),
        grid=(bsz, l // tm),
        in_specs=[pl.BlockSpec((None, halo, B_CH), lambda i, t: (i, jnp.maximum(t * r - 1, 0), 0)),
                  pl.BlockSpec((None, tm, B_CH), lambda i, t: (i, t, 0)),
                  pl.BlockSpec(w.shape, vec), pl.BlockSpec((1, B_CH), vec), pl.BlockSpec((1, B_CH), vec),
                  pl.BlockSpec((1, B_CH), vec)],
        out_specs=pl.BlockSpec((None, tm, B_CH), lambda i, t: (i, t, 0)),
        out_shape=jax.ShapeDtypeStruct((bsz, l, B_CH), F32),
        scratch_shapes=[pltpu.VMEM((tm + halo, B_CH), F32)],
        compiler_params=_params(("parallel", "arbitrary")),
        name="conv_prompt",
    )(u, u, w, b[None], g[None], bb[None])


def _conv_sample_body(up_ref, w_ref, b_ref, g_ref, bb_ref, o_ref, *, bt, s_len):
    kw = w_ref.shape[0]
    acc = jnp.zeros((bt, s_len, B_CH), F32) + b_ref[...]
    for j in range(kw):
        acc = acc + w_ref[j:j + 1, :] * up_ref[:, pl.ds(j, s_len), :]
    o_ref[...] = _silu(_ln(acc, g_ref[...], bb_ref[...]))


def _conv_sample(up, w, b, g, bb, bt=32):
    bsz, rows, _ = up.shape
    bt = min(bt, bsz)
    s_len = rows - (w.shape[0] - 1)
    vec = lambda i: (0, 0)
    return pl.pallas_call(
        functools.partial(_conv_sample_body, bt=bt, s_len=s_len),
        grid=(bsz // bt,),
        in_specs=[pl.BlockSpec((bt, rows, B_CH), lambda i: (i, 0, 0)),
                  pl.BlockSpec(w.shape, vec), pl.BlockSpec((1, B_CH), vec), pl.BlockSpec((1, B_CH), vec),
                  pl.BlockSpec((1, B_CH), vec)],
        out_specs=pl.BlockSpec((bt, s_len, B_CH), lambda i: (i, 0, 0)),
        out_shape=jax.ShapeDtypeStruct((bsz, s_len, B_CH), F32),
        compiler_params=_params(("parallel",)),
        name="conv_sample",
    )(up, w, b[None], g[None], bb[None])


def _hgrn_tile(hq, df, v, lb_ref, ch):
    n = LANES // ch
    loglb, log1mlb, onemlb = lb_ref[0:1, :], lb_ref[1:2, :], lb_ref[2:3, :]
    e = jnp.exp(-jnp.abs(df))
    ls = jnp.minimum(df, 0.0) - jnp.log1p(e)
    b_ = log1mlb + ls
    mx = jnp.maximum(loglb, b_)
    logf = mx + jnp.log(jnp.exp(loglb - mx) + jnp.exp(b_ - mx))
    hk = onemlb * (jnp.where(df >= 0.0, e, 1.0) / (1.0 + e))
    row = lax.broadcasted_iota(jnp.int32, (LANES, LANES), 0)
    colv = lax.broadcasted_iota(jnp.int32, (LANES, LANES), 1)
    r = row % ch
    g = logf
    sh = 1
    while sh < ch:
        g = g + jnp.where(r >= sh, pltpu.roll(g, sh, 0), 0.0)
        sh *= 2
    qa, kb, kd, qe, dec = [], [], [], [], []
    for c in range(n):
        sl = slice(ch * c, ch * (c + 1))
        gc = g[sl]
        gm = gc[ch // 2 - 1:ch // 2]
        gl = gc[ch - 1:ch]
        qa.append(hq[sl] * jnp.exp(gc - gm))
        kb.append(hk[sl] * jnp.exp(gm - gc))
        kd.append(hk[sl] * jnp.exp(gl - gc))
        qe.append((hq[sl] * jnp.exp(gc)).astype(BF16))
        dec.append(jnp.exp(gl))
    att = _nt(jnp.concatenate(qa, 0).astype(BF16), jnp.concatenate(kb, 0).astype(BF16))
    att = jnp.where((row // ch == colv // ch) & (colv <= row), att, 0.0)
    vb = v.astype(BF16)
    o_intra = _nn(att.astype(BF16), vb)
    vt = v.T
    kdb = jnp.concatenate(kd, 0).astype(BF16)
    ut = [_nn(jnp.where(colv // ch == c, vt, 0.0).astype(BF16), kdb) for c in range(n)]
    return o_intra, ut, qe, dec


def _hgrn_finish(o, sg, gn):
    return o * lax.rsqrt(jnp.mean(o * o, axis=-1, keepdims=True) + LN_EPS) * gn * sg


def _hgrn_prompt_body(hq_ref, df_ref, di_ref, sg_ref, lb_ref, gn_ref, o_ref, s_ref, *, ngroups, ch):
    def group(gi, st):
        off = pl.multiple_of(gi * LANES, LANES)
        rows = pl.ds(off, LANES)
        o_intra, ut, qe, dec = _hgrn_tile(hq_ref[rows, :], df_ref[rows, :], di_ref[rows, :], lb_ref, ch)
        o_inter = []
        for c in range(LANES // ch):
            o_inter.append(_nt(qe[c], st.astype(BF16)))
            st = dec[c] * st + ut[c]
        o = o_intra + jnp.concatenate(o_inter, 0)
        o_ref[rows, :] = _hgrn_finish(o, sg_ref[rows, :], gn_ref[...])
        return st

    st = lax.fori_loop(0, ngroups, group, jnp.zeros((LANES, LANES), F32))
    s_ref[...] = st.T


def _hgrn_prompt(d4, lbt, gn, ch=16):
    b, l, _ = d4.shape
    seg = lambda s: pl.BlockSpec((None, l, LANES), lambda i, h: (i, 0, D_HEADS * s + h))
    return pl.pallas_call(
        functools.partial(_hgrn_prompt_body, ngroups=l // LANES, ch=ch),
        grid=(b, D_HEADS),
        in_specs=[seg(0), seg(1), seg(2), seg(3),
                  pl.BlockSpec((3, LANES), lambda i, h: (0, h)), pl.BlockSpec((1, LANES), lambda i, h: (0, 0))],
        out_specs=[pl.BlockSpec((None, l, LANES), lambda i, h: (i, 0, h)),
                   pl.BlockSpec((None, None, D_DK, LANES), lambda i, h: (i, h, 0, 0))],
        out_shape=[jax.ShapeDtypeStruct((b, l, D_HEADS * LANES), F32),
                   jax.ShapeDtypeStruct((b, D_HEADS, D_DK, LANES), F32)],
        compiler_params=_params(("parallel", "arbitrary")),
        name="hgrn_prompt",
    )(d4, d4, d4, d4, lbt, gn[None])


def _hgrn_sample_body(hq_ref, df_ref, di_ref, sg_ref, lb_ref, gn_ref, s0_ref, o_ref, s_ref, *, ch):
    o_intra, ut, qe, dec = _hgrn_tile(hq_ref[...], df_ref[...], di_ref[...], lb_ref, ch)
    o_inter = []
    for c in range(LANES // ch):
        st = s0_ref[c].T
        o_inter.append(_nt(qe[c], st.astype(BF16)))
        s_ref[c] = (dec[c] * st + ut[c]).T
    o = o_intra + jnp.concatenate(o_inter, 0)
    o_ref[...] = _hgrn_finish(o, sg_ref[...], gn_ref[...])


def _hgrn_sample(d4, lbt, gn, s0):
    m = d4.shape[0]
    b = s0.shape[0]
    s_len = m // b
    assert LANES % s_len == 0 and s_len & (s_len - 1) == 0 and m % LANES == 0
    bt = LANES // s_len
    seg = lambda s: pl.BlockSpec((LANES, LANES), lambda i, h: (i, D_HEADS * s + h))
    st_spec = pl.BlockSpec((bt, None, D_DK, LANES), lambda i, h: (i, h, 0, 0))
    return pl.pallas_call(
        functools.partial(_hgrn_sample_body, ch=s_len),
        grid=(m // LANES, D_HEADS),
        in_specs=[seg(0), seg(1), seg(2), seg(3),
                  pl.BlockSpec((3, LANES), lambda i, h: (0, h)), pl.BlockSpec((1, LANES), lambda i, h: (0, 0)), st_spec],
        out_specs=[pl.BlockSpec((LANES, LANES), lambda i, h: (i, h)), st_spec],
        out_shape=[jax.ShapeDtypeStruct((m, D_HEADS * LANES), F32), jax.ShapeDtypeStruct(s0.shape, F32)],
        compiler_params=_params(("parallel", "arbitrary")),
        name="hgrn_sample",
    )(d4, d4, d4, d4, lbt, gn[None], s0)


def _outproj_body(a_ref, c_ref, x_ref, w_ref, g_ref, b_ref, o_ref):
    wa = a_ref.shape[1]
    y = _nn(a_ref[...].astype(BF16), w_ref[0:wa, :]) + _nn(c_ref[...].astype(BF16), w_ref[wa:, :])
    o_ref[...] = _ln(ALPHA * x_ref[...] + y, g_ref[...], b_ref[...])


def _outproj_ln(a, c, x2, w_bf, g, b, tm):
    m = x2.shape[0]
    row = lambda i: (i, 0)
    vec = lambda i: (0, 0)
    return pl.pallas_call(
        _outproj_body,
        grid=(m // tm,),
        in_specs=[pl.BlockSpec((tm, a.shape[1]), row), pl.BlockSpec((tm, c.shape[1]), row),
                  pl.BlockSpec((tm, D_MODEL), row), pl.BlockSpec(w_bf.shape, vec),
                  pl.BlockSpec((1, D_MODEL), vec), pl.BlockSpec((1, D_MODEL), vec)],
        out_specs=pl.BlockSpec((tm, D_MODEL), row),
        out_shape=jax.ShapeDtypeStruct((m, D_MODEL), F32),
        compiler_params=_params(("parallel",)),
        name="outproj_ln",
    )(a, c, x2, w_bf, g[None], b[None])


def _ffn_body(x_ref, halo_ref, hist_ref, wg_ref, wu_ref, wd_ref, wc_ref, bc_ref, g_ref, b_ref,
              o_ref, st_ref, *, tm, seg, fc):
    x = x_ref[...]
    xb = x.astype(BF16)
    row = lax.broadcasted_iota(jnp.int32, (tm, fc), 0)
    r = row % seg
    acc = jnp.zeros((tm, D_MODEL), F32)
    for c in range(D_FF // fc):
        cols = slice(fc * c, fc * (c + 1))
        gp = _nn(xb, wg_ref[:, cols])
        if seg == tm:
            t = pl.program_id(1)
            hg = _nn(halo_ref[...].astype(BF16), wg_ref[:, cols])
            h0 = jnp.where(t == 0, hist_ref[0:1, cols], hg[6:7])
            h1 = jnp.where(t == 0, hist_ref[1:2, cols], hg[7:8])
            p1 = jnp.where(row == 0, h1, pltpu.roll(gp, 1, 0))
            p2 = jnp.where(row == 0, h0, jnp.where(row == 1, h1, pltpu.roll(gp, 2, 0)))
            st_ref[:, cols] = gp[tm - 2:tm]
        else:
            hp = hist_ref[:, cols]
            p1 = jnp.where(r == 0, pltpu.roll(hp, (tm - (seg - 1)) % tm, 0), pltpu.roll(gp, 1, 0))
            p2 = jnp.where(r < 2, pltpu.roll(hp, (tm - (seg - 2)) % tm, 0), pltpu.roll(gp, 2, 0))
            st_ref[:, cols] = gp
        cv = wc_ref[0:1, cols] * p2 + wc_ref[1:2, cols] * p1 + wc_ref[2:3, cols] * gp + bc_ref[:, cols]
        hid = 0.5 * cv * (1.0 + lax.erf(cv * (2.0 ** -0.5))) * _nn(xb, wu_ref[:, cols])
        acc = acc + _nn(hid.astype(BF16), wd_ref[cols, :])
    o_ref[...] = _ln(ALPHA * x + acc, g_ref[...], b_ref[...])


def _ffn_prompt(x, hist, wg, wu, wd, wc, bc, g, b, tm=512, fc=256):
    bsz, l, _ = x.shape
    vec = lambda i, t: (0, 0)
    r = tm // 8
    return pl.pallas_call(
        functools.partial(_ffn_body, tm=tm, seg=tm, fc=fc),
        grid=(bsz, l // tm),
        in_specs=[pl.BlockSpec((None, tm, D_MODEL), lambda i, t: (i, t, 0)),
                  pl.BlockSpec((None, 8, D_MODEL), lambda i, t: (i, jnp.maximum(t * r - 1, 0), 0)),
                  pl.BlockSpec((None, 2, D_FF), lambda i, t: (i, 0, 0)),
                  pl.BlockSpec(wg.shape, vec), pl.BlockSpec(wu.shape, vec), pl.BlockSpec(wd.shape, vec),
                  pl.BlockSpec(wc.shape, vec), pl.BlockSpec((1, D_FF), vec),
                  pl.BlockSpec((1, D_MODEL), vec), pl.BlockSpec((1, D_MODEL), vec)],
        out_specs=[pl.BlockSpec((None, tm, D_MODEL), lambda i, t: (i, t, 0)),
                   pl.BlockSpec((None, 2, D_FF), lambda i, t: (i, 0, 0))],
        out_shape=[jax.ShapeDtypeStruct((bsz, l, D_MODEL), F32), jax.ShapeDtypeStruct((bsz, 2, D_FF), F32)],
        compiler_params=_params(("parallel", "arbitrary")),
        name="ffn_prompt",
    )(x, x, hist, wg, wu, wd, wc, bc[None], g[None], b[None])


def _ffn_sample(x2, hist_rows, s_len, wg, wu, wd, wc, bc, g, b, tm=512, fc=256):
    m = x2.shape[0]
    tm = min(tm, m)
    vec = lambda i: (0, 0)
    row = lambda i: (i, 0)
    return pl.pallas_call(
        functools.partial(_ffn_body, tm=tm, seg=s_len, fc=fc),
        grid=(m // tm,),
        in_specs=[pl.BlockSpec((tm, D_MODEL), row), pl.BlockSpec((8, D_MODEL), vec), pl.BlockSpec((tm, D_FF), row),
                  pl.BlockSpec(wg.shape, vec), pl.BlockSpec(wu.shape, vec), pl.BlockSpec(wd.shape, vec),
                  pl.BlockSpec(wc.shape, vec), pl.BlockSpec((1, D_FF), vec),
                  pl.BlockSpec((1, D_MODEL), vec), pl.BlockSpec((1, D_MODEL), vec)],
        out_specs=[pl.BlockSpec((tm, D_MODEL), row), pl.BlockSpec((tm, D_FF), row)],
        out_shape=[jax.ShapeDtypeStruct((m, D_MODEL), F32), jax.ShapeDtypeStruct((m, D_FF), F32)],
        compiler_params=_params(("parallel",)),
        name="ffn_sample",
    )(x2, x2, hist_rows, wg, wu, wd, wc, bc[None], g[None], b[None])


def _lb_tables(lb_param):
    lbs = jnp.cumsum(jax.nn.softmax(lb_param.astype(F32), axis=0), axis=0)
    lbs = lbs - lbs[:1]
    return jnp.stack([jnp.log(lbs), jnp.log1p(-lbs), 1.0 - lbs], axis=1)


def _trunk_prompt(x, prm, lbt):
    b, l, _ = x.shape
    m = b * l
    tm = 512
    tabs = _rope_tables(jnp.arange(l))
    kv_k, kv_v, conv_b, win_k, win_v, hgrn, ffn = [], [], [], [], [], [], []
    zero_hist = jnp.zeros((b, FFN_CONV_WIDTH - 1, D_FF), F32)
    for layer in range(DEPTH):
        i = layer // 2
        x2 = x.reshape(m, D_MODEL)
        if layer % 2 == 0:
            q, k, v, u = _inproj(x2, prm["w_in_even"][i], tabs, True, tm)
            k3, v3, u3 = k.reshape(b, l, KVW), v.reshape(b, l, KVW), u.reshape(b, l, B_CH)
            a = _moba_prompt(q.reshape(b, l, QW), k3, v3)
            cv = _conv_prompt(u3, prm["w_dw_b"][i], prm["b_dw_b"][i], prm["conv_ln_g"][i], prm["conv_ln_b"][i])
            kv_k.append(k3.reshape(b, l, N_KV, HEAD_DIM))
            kv_v.append(v3.reshape(b, l, N_KV, HEAD_DIM))
            conv_b.append(u3[:, l - (B_CONV_WIDTH - 1):])
            x1 = _outproj_ln(a.reshape(m, QW), cv.reshape(m, B_CH), x2, prm["w_out_even"][i],
                             prm["ln1_g"][layer], prm["ln1_b"][layer], tm)
        else:
            q, k, v, d4 = _inproj(x2, prm["w_in_odd"][i], tabs, False, tm)
            k3, v3 = k.reshape(b, l, KVW), v.reshape(b, l, KVW)
            a = _swa_prompt(q.reshape(b, l, QW), k3, v3, prm["sinks"][i])
            o, st = _hgrn_prompt(d4.reshape(b, l, 2048), lbt[i], prm["gnorm_g"][i])
            keep = min(WINDOW, l)
            win_k.append(k3[:, l - keep:].reshape(b, keep, N_KV, HEAD_DIM))
            win_v.append(v3[:, l - keep:].reshape(b, keep, N_KV, HEAD_DIM))
            hgrn.append(st)
            x1 = _outproj_ln(a.reshape(m, QW), o.reshape(m, 512), x2, prm["w_out_odd"][i],
                             prm["ln1_g"][layer], prm["ln1_b"][layer], tm)
        xo, fb = _ffn_prompt(x1.reshape(b, l, D_MODEL), zero_hist, prm["w_ffn_gate"][layer], prm["w_ffn_up"][layer],
                             prm["w_ffn_down"][layer], prm["w_ffn_conv"][layer], prm["b_ffn_conv"][layer],
                             prm["ln2_g"][layer], prm["ln2_b"][layer])
        ffn.append(fb)
        x = xo
    return (x, jnp.stack(kv_k), jnp.stack(kv_v), jnp.stack(conv_b), jnp.stack(win_k), jnp.stack(win_v),
            jnp.stack(hgrn), jnp.stack(ffn))


def _trunk_sample(x, prm, lbt, cache_k, cache_v, page_table, state_conv_b, state_win_k, state_win_v,
                  state_hgrn, state_ffn):
    b, s, _ = x.shape
    m = b * s
    tm = min(512, m)
    past = page_table.shape[1] * PAGE_SIZE
    tabs = _rope_tables(jnp.tile(past + jnp.arange(s), tm // s))
    n_pool = cache_k.shape[1]
    kv_k, kv_v, conv_b, win_k, win_v, hgrn, ffn = [], [], [], [], [], [], []
    for layer in range(DEPTH):
        i = layer // 2
        x2 = x.reshape(m, D_MODEL)
        if layer % 2 == 0:
            q, k, v, u = _inproj(x2, prm["w_in_even"][i], tabs, True, tm)
            k3, v3 = k.reshape(b, s, KVW), v.reshape(b, s, KVW)
            a = _moba_sample(q.reshape(b, s, QW), k3, v3, cache_k[i].reshape(n_pool, PAGE_SIZE, KVW),
                             cache_v[i].reshape(n_pool, PAGE_SIZE, KVW), page_table)
            up = jnp.concatenate([state_conv_b[i], u.reshape(b, s, B_CH)], 1)
            cv = _conv_sample(up, prm["w_dw_b"][i], prm["b_dw_b"][i], prm["conv_ln_g"][i], prm["conv_ln_b"][i])
            kv_k.append(k3.reshape(b, s, N_KV, HEAD_DIM))
            kv_v.append(v3.reshape(b, s, N_KV, HEAD_DIM))
            conv_b.append(up[:, up.shape[1] - (B_CONV_WIDTH - 1):])
            x1 = _outproj_ln(a.reshape(m, QW), cv.reshape(m, B_CH), x2, prm["w_out_even"][i],
                             prm["ln1_g"][layer], prm["ln1_b"][layer], tm)
        else:
            q, k, v, d4 = _inproj(x2, prm["w_in_odd"][i], tabs, False, tm)
            k3, v3 = k.reshape(b, s, KVW), v.reshape(b, s, KVW)
            wk = state_win_k[i].reshape(b, -1, KVW)
            wv = state_win_v[i].reshape(b, -1, KVW)
            a = _swa_sample(q.reshape(b, s, QW), k3, v3, wk, wv, prm["sinks"][i])
            o, st = _hgrn_sample(d4, lbt[i], prm["gnorm_g"][i], state_hgrn[i])
            kk = jnp.concatenate([wk, k3], 1)
            vv = jnp.concatenate([wv, v3], 1)
            keep = min(WINDOW, past + s)
            win_k.append(kk[:, kk.shape[1] - keep:].reshape(b, keep, N_KV, HEAD_DIM))
            win_v.append(vv[:, vv.shape[1] - keep:].reshape(b, keep, N_KV, HEAD_DIM))
            hgrn.append(st)
            x1 = _outproj_ln(a.reshape(m, QW), o, x2, prm["w_out_odd"][i],
                             prm["ln1_g"][layer], prm["ln1_b"][layer], tm)
        hist_rows = jnp.pad(state_ffn[layer], ((0, 0), (s - (FFN_CONV_WIDTH - 1), 0), (0, 0))).reshape(m, D_FF)
        xo, gp = _ffn_sample(x1, hist_rows, s, prm["w_ffn_gate"][layer], prm["w_ffn_up"][layer],
                             prm["w_ffn_down"][layer], prm["w_ffn_conv"][layer], prm["b_ffn_conv"][layer],
                             prm["ln2_g"][layer], prm["ln2_b"][layer], tm=tm)
        ffn.append(gp.reshape(b, s, D_FF)[:, s - (FFN_CONV_WIDTH - 1):])
        x = xo.reshape(b, s, D_MODEL)
    return (x, jnp.stack(kv_k), jnp.stack(kv_v), jnp.stack(conv_b), jnp.stack(win_k), jnp.stack(win_v),
            jnp.stack(hgrn), jnp.stack(ffn))


def kernel(x_prompt, x_sample, cache_k, cache_v, state_conv_b, state_win_k, state_win_v, state_hgrn, state_ffn,
           page_table, w_in_even, w_dw_b, b_dw_b, conv_ln_g, conv_ln_b, w_out_even, w_in_odd, sinks, lb_param,
           gnorm_g, w_out_odd, ln1_g, ln1_b, ln2_g, ln2_b, w_ffn_gate, w_ffn_up, w_ffn_conv, b_ffn_conv, w_ffn_down):
    prm = dict(w_in_even=w_in_even.astype(BF16), w_dw_b=w_dw_b, b_dw_b=b_dw_b, conv_ln_g=conv_ln_g,
               conv_ln_b=conv_ln_b, w_out_even=w_out_even.astype(BF16), w_in_odd=w_in_odd.astype(BF16), sinks=sinks,
               gnorm_g=gnorm_g, w_out_odd=w_out_odd.astype(BF16), ln1_g=ln1_g, ln1_b=ln1_b, ln2_g=ln2_g, ln2_b=ln2_b,
               w_ffn_gate=w_ffn_gate.astype(BF16), w_ffn_up=w_ffn_up.astype(BF16), w_ffn_conv=w_ffn_conv,
               b_ffn_conv=b_ffn_conv, w_ffn_down=w_ffn_down.astype(BF16))
    lbt = _lb_tables(lb_param)
    yp, kkp, kvp, cbp, wkp, wvp, hgp, ffp = _trunk_prompt(x_prompt, prm, lbt)
    ys, kks, kvs, cbs, wks, wvs, hgs, ffs = _trunk_sample(
        x_sample, prm, lbt, cache_k, cache_v, page_table, state_conv_b, state_win_k, state_win_v, state_hgrn, state_ffn)
    return (yp, ys, kkp, kvp, kks, kvs, cbp, cbs, wkp, wvp, wks, wvs, hgp, hgs, ffp, ffs)
```

```python
import functools

import jax
import jax.numpy as jnp
from jax import lax
from jax.experimental import pallas as pl
from jax.experimental.pallas import tpu as pltpu

F32 = jnp.float32
BF16 = jnp.bfloat16

D_MODEL = 1024
DEPTH = 4
N_PAIR = DEPTH // 2
HEAD_DIM = 64
ROT_DIM = HEAD_DIM // 4
ROPE_THETA = 500000.0
N_HEADS = 8
N_KV = 2
GROUP = N_HEADS // N_KV
QW = N_HEADS * HEAD_DIM
KVW = N_KV * HEAD_DIM
MOBA_BLOCK = 256
MOBA_TOPK = 3
Q_CHUNK = 128
PAGE_SIZE = 128
B_CH = 512
B_CONV_WIDTH = 31
WINDOW = 128
D_HEADS = 4
D_DK = 128
D_FF = 2816
FFN_CONV_WIDTH = 3
LN_EPS = 1e-5
ALPHA = (2.0 * DEPTH) ** 0.25
SCALE = HEAD_DIM ** -0.5
EVEN_IN = QW + 2 * KVW + 2 * B_CH
ODD_IN = QW + 2 * KVW + 4 * 512

LANES = 128
NEG = -1e30
VMEM_LIMIT = 56 * 1024 * 1024

HIGHEST = lax.Precision.HIGHEST


def _nn(a, b, precision=None):
    return lax.dot_general(a, b, (((1,), (0,)), ((), ())), preferred_element_type=F32, precision=precision)


def _nt(a, b, precision=None):
    return lax.dot_general(a, b, (((1,), (1,)), ((), ())), preferred_element_type=F32, precision=precision)


def _ln(z, g, b):
    mu = jnp.mean(z, axis=-1, keepdims=True)
    d = z - mu
    var = jnp.mean(d * d, axis=-1, keepdims=True)
    return d * lax.rsqrt(var + LN_EPS) * g + b


def _silu(x):
    return x * jax.nn.sigmoid(x)


def _params(sem):
    return pltpu.CompilerParams(dimension_semantics=sem, vmem_limit_bytes=VMEM_LIMIT)


def _rope_tables(pos):
    half = ROT_DIM // 2
    inv = ROPE_THETA ** (-jnp.arange(half, dtype=F32) * 2.0 / ROT_DIM)
    ang = pos.astype(F32)[:, None] * inv[None, :]
    cos, sin = jnp.cos(ang), jnp.sin(ang)
    p = pos.shape[0]
    one = jnp.ones((p, HEAD_DIM - ROT_DIM), F32)
    zero = jnp.zeros((p, HEAD_DIM - ROT_DIM), F32)
    zh = jnp.zeros((p, half), F32)
    c = jnp.concatenate([cos, cos, one], 1)
    sa = jnp.concatenate([-sin, zh, zero], 1)
    sb = jnp.concatenate([zh, sin, zero], 1)
    return tuple(jnp.tile(t, (1, LANES // HEAD_DIM)) for t in (c, sa, sb))


def _rope(z, c, sa, sb):
    return z * c + pltpu.roll(z, LANES - ROT_DIM // 2, 1) * sa + pltpu.roll(z, ROT_DIM // 2, 1) * sb


def _inproj_even_body(x_ref, w_ref, c_ref, sa_ref, sb_ref, q_ref, k_ref, v_ref, u_ref):
    xb = x_ref[...].astype(BF16)
    c, sa, sb = c_ref[...], sa_ref[...], sb_ref[...]
    hq = _nn(xb, w_ref[:, 0:QW])
    for s in range(QW // LANES):
        q_ref[:, LANES * s:LANES * (s + 1)] = _rope(hq[:, LANES * s:LANES * (s + 1)], c, sa, sb)
    hkv = _nn(xb, w_ref[:, QW:QW + 2 * KVW])
    k_ref[...] = _rope(hkv[:, 0:KVW], c, sa, sb)
    v_ref[...] = hkv[:, KVW:2 * KVW]
    o = QW + 2 * KVW
    ga = _nn(xb, w_ref[:, o:o + B_CH])
    gb = _nn(xb, w_ref[:, o + B_CH:o + 2 * B_CH])
    u_ref[...] = ga * jax.nn.sigmoid(gb)


def _inproj_odd_body(x_ref, w_ref, c_ref, sa_ref, sb_ref, q_ref, k_ref, v_ref, d_ref):
    xb = x_ref[...].astype(BF16)
    c, sa, sb = c_ref[...], sa_ref[...], sb_ref[...]
    hq = _nn(xb, w_ref[:, 0:QW])
    for s in range(QW // LANES):
        q_ref[:, LANES * s:LANES * (s + 1)] = _rope(hq[:, LANES * s:LANES * (s + 1)], c, sa, sb)
    hkv = _nn(xb, w_ref[:, QW:QW + 2 * KVW])
    k_ref[...] = _rope(hkv[:, 0:KVW], c, sa, sb)
    v_ref[...] = hkv[:, KVW:2 * KVW]
    o = QW + 2 * KVW
    d_ref[:, 0:512] = _silu(_nn(xb, w_ref[:, o:o + 512]))
    d_ref[:, 512:1536] = _nn(xb, w_ref[:, o + 512:o + 1536])
    d_ref[:, 1536:2048] = _silu(_nn(xb, w_ref[:, o + 1536:o + 2048]))


def _inproj(x2, w_bf, tabs, even, tm):
    m = x2.shape[0]
    n_in = w_bf.shape[1]
    ntab = tabs[0].shape[0] // tm
    row = lambda i: (i, 0)
    tab = lambda i: (i % ntab, 0)
    wide = B_CH if even else 2048
    return pl.pallas_call(
        _inproj_even_body if even else _inproj_odd_body,
        grid=(m // tm,),
        in_specs=[pl.BlockSpec((tm, D_MODEL), row), pl.BlockSpec((D_MODEL, n_in), lambda i: (0, 0)),
                  pl.BlockSpec((tm, LANES), tab), pl.BlockSpec((tm, LANES), tab), pl.BlockSpec((tm, LANES), tab)],
        out_specs=[pl.BlockSpec((tm, QW), row), pl.BlockSpec((tm, KVW), row), pl.BlockSpec((tm, KVW), row),
                   pl.BlockSpec((tm, wide), row)],
        out_shape=[jax.ShapeDtypeStruct((m, QW), F32), jax.ShapeDtypeStruct((m, KVW), F32),
                   jax.ShapeDtypeStruct((m, KVW), F32), jax.ShapeDtypeStruct((m, wide), F32)],
        compiler_params=_params(("parallel",)),
        name="inproj_even" if even else "inproj_odd",
    )(x2, w_bf, *tabs)


def _queries_t(q):
    return jnp.concatenate([q[:, LANES * s:LANES * (s + 1)].T for s in range(QW // LANES)], 0)


def _stack_heads_t(qt, j):
    t = qt.shape[1]
    z = jnp.zeros((HEAD_DIM, t), F32)
    blocks = []
    for g in range(GROUP):
        h = GROUP * j + g
        hd = qt[HEAD_DIM * h:HEAD_DIM * (h + 1), :]
        blocks.append(jnp.concatenate([hd, z] if j == 0 else [z, hd], 0))
    return jnp.concatenate(blocks, 1)


def _unstack_heads(outs, t):
    lane = lax.broadcasted_iota(jnp.int32, (t, LANES), 1)
    slabs = []
    for p in range(N_HEADS // 2):
        halves = []
        for hh in range(2):
            h = 2 * p + hh
            j, g = h // GROUP, h % GROUP
            blk = outs[j][g * t:(g + 1) * t]
            if j != hh:
                blk = pltpu.roll(blk, HEAD_DIM, 1)
            halves.append(blk)
        slabs.append(jnp.where(lane < HEAD_DIM, halves[0], halves[1]))
    return jnp.concatenate(slabs, 1)


def _untranspose(o_t, t):
    return jnp.concatenate([o_t[:, t * g:t * (g + 1)].T for g in range(GROUP)], 0)


def _moba_prompt_body(q_ref, k_ref, v_ref, o_ref, km_ref, vt_ref, sel_ref, *, nb):
    c = pl.program_id(1)
    blk = c // (MOBA_BLOCK // Q_CHUNK)
    qs0 = c * Q_CHUNK
    cols = GROUP * Q_CHUNK
    nbp = km_ref.shape[0]

    @pl.when(c == 0)
    def _():
        km_ref[...] = jnp.concatenate(
            [jnp.mean(k_ref[n * MOBA_BLOCK:(n + 1) * MOBA_BLOCK, :], axis=0, keepdims=True) for n in range(nb)]
            + [jnp.zeros((nbp - nb, LANES), F32)] * (nbp > nb), 0)
        for n in range(nb):
            vt_ref[n] = v_ref[n * MOBA_BLOCK:(n + 1) * MOBA_BLOCK, :].T.astype(BF16)

    qt = _queries_t(q_ref[...] * SCALE)
    blk_row = lax.broadcasted_iota(jnp.int32, (nbp, cols), 0)
    kidx = lax.broadcasted_iota(jnp.int32, (MOBA_BLOCK, cols), 0)
    qpos = qs0 + lax.broadcasted_iota(jnp.int32, (MOBA_BLOCK, cols), 1) % Q_CHUNK
    outs = []
    for j in range(N_KV):
        qs_t = _stack_heads_t(qt, j)
        qsb = qs_t.astype(BF16)
        gate = jnp.where(blk_row < blk, _nn(km_ref[...], qs_t, HIGHEST), -jnp.inf)
        sel_rows = []
        for n in range(nb):
            gn = gate[n:n + 1, :]
            beats = jnp.where((gate > gn) | ((gate == gn) & (blk_row < n)), 1.0, 0.0)
            cnt = jnp.sum(beats, axis=0, keepdims=True)
            sel_rows.append(jnp.where((cnt < MOBA_TOPK) & (n < blk), 1.0, 0.0))
        sel_ref[...] = jnp.concatenate(sel_rows + [jnp.zeros((nbp - nb, cols), F32)] * (nbp > nb), 0)

        def scores(n):
            off = pl.multiple_of(n * MOBA_BLOCK, MOBA_BLOCK)
            return _nn(k_ref[pl.ds(off, MOBA_BLOCK), :].astype(BF16), qsb), vt_ref[n]

        s, vt = scores(blk)
        s = jnp.where(kidx + blk * MOBA_BLOCK <= qpos, s, NEG)
        m = jnp.max(s, axis=0, keepdims=True)
        p = jnp.exp(s - m)
        l = jnp.sum(p, axis=0, keepdims=True)
        acc = _nn(vt, p.astype(BF16))

        def past(n, carry):
            m, l, acc = carry
            s, vt = scores(n)
            s = jnp.where(sel_ref[pl.ds(n, 1), :] > 0.5, s, NEG)
            m_new = jnp.maximum(m, jnp.max(s, axis=0, keepdims=True))
            a = jnp.exp(m - m_new)
            p = jnp.exp(s - m_new)
            return m_new, a * l + jnp.sum(p, axis=0, keepdims=True), a * acc + _nn(vt, p.astype(BF16))

        m, l, acc = lax.fori_loop(0, blk, past, (m, l, acc))
        outs.append(_untranspose(acc / l, Q_CHUNK))
    o_ref[...] = _unstack_heads(outs, Q_CHUNK)


def _moba_prompt(q, k, v):
    b, l, _ = q.shape
    nb = l // MOBA_BLOCK
    nbp = -(-nb // 8) * 8
    return pl.pallas_call(
        functools.partial(_moba_prompt_body, nb=nb),
        grid=(b, l // Q_CHUNK),
        in_specs=[pl.BlockSpec((None, Q_CHUNK, QW), lambda i, c: (i, c, 0)),
                  pl.BlockSpec((None, l, KVW), lambda i, c: (i, 0, 0)),
                  pl.BlockSpec((None, l, KVW), lambda i, c: (i, 0, 0))],
        out_specs=pl.BlockSpec((None, Q_CHUNK, QW), lambda i, c: (i, c, 0)),
        out_shape=jax.ShapeDtypeStruct((b, l, QW), F32),
        scratch_shapes=[pltpu.VMEM((nbp, LANES), F32), pltpu.VMEM((nb, KVW, MOBA_BLOCK), BF16),
                        pltpu.VMEM((nbp, GROUP * Q_CHUNK), F32)],
        compiler_params=_params(("parallel", "arbitrary")),
        name="moba_prompt",
    )(q, k, v)


def _stack_decode(q):
    s = q.shape[0]
    lane = lax.broadcasted_iota(jnp.int32, (s, LANES), 1)
    parts = []
    for h in range(N_HEADS):
        j = h // GROUP
        slab = q[:, LANES * (h // 2):LANES * (h // 2 + 1)]
        if h % 2 != j:
            slab = pltpu.roll(slab, HEAD_DIM, 1)
        keep = (lane < HEAD_DIM) if j == 0 else (lane >= HEAD_DIM)
        parts.append(jnp.where(keep, slab, 0.0))
    return jnp.concatenate(parts, 0)


def _unstack_decode(o, s):
    lane = lax.broadcasted_iota(jnp.int32, (s, LANES), 1)
    slabs = []
    for p in range(N_HEADS // 2):
        halves = []
        for hh in range(2):
            h = 2 * p + hh
            blk = o[h * s:(h + 1) * s]
            if h // GROUP != hh:
                blk = pltpu.roll(blk, HEAD_DIM, 1)
            halves.append(blk)
        slabs.append(jnp.where(lane < HEAD_DIM, halves[0], halves[1]))
    return jnp.concatenate(slabs, 1)


def _pad_rows(x, n):
    return jnp.concatenate([x, jnp.zeros((n - x.shape[0], x.shape[1]), x.dtype)], 0)


def _kv_transposed(x):
    lead = x.shape[:-3]
    n = len(lead)
    t = x.shape[-3]
    return jnp.transpose(x, tuple(range(n)) + (n + 1, n + 2, n)).reshape(lead + (KVW, t))


def _kv_untransposed(xt):
    lead = xt.shape[:-2]
    n = len(lead)
    t = xt.shape[-1]
    return jnp.transpose(xt.reshape(lead + (N_KV, HEAD_DIM, t)), tuple(range(n)) + (n + 2, n, n + 1))


def _moba_sample_body(pt_ref, q_ref, kn_ref, vn_ref, *rest, nb, s_len):
    ppb = MOBA_BLOCK // PAGE_SIZE
    npages = nb * ppb
    kp, vp = rest[:npages], rest[npages:2 * npages]
    o_ref, s_ref = rest[2 * npages], rest[2 * npages + 1]
    rows = N_HEADS * s_len
    qa = _stack_decode(q_ref[...] * SCALE)
    qab = qa.astype(BF16)
    lane_k = lax.broadcasted_iota(jnp.int32, (KVW, LANES), 1)
    km_t = jnp.zeros((KVW, LANES), F32)
    for n in range(nb):
        tot = kp[ppb * n][...]
        for r in range(1, ppb):
            tot = tot + kp[ppb * n + r][...]
        km_t = jnp.where(lane_k == n, jnp.sum(tot, axis=1, keepdims=True) * (1.0 / MOBA_BLOCK), km_t)
    lane = lax.broadcasted_iota(jnp.int32, (rows, LANES), 1)
    gate = jnp.where(lane < nb, _nn(qa, km_t, HIGHEST), -jnp.inf)
    tok = lax.broadcasted_iota(jnp.int32, (rows, LANES), 0) % s_len
    s_new = jnp.where((lane <= tok) & (lane < s_len), _nt(qab, _pad_rows(kn_ref[...], LANES).astype(BF16)), NEG)
    m = jnp.max(s_new, axis=1, keepdims=True)
    for n in range(nb):
        gn = gate[:, n:n + 1]
        beats = jnp.where((gate > gn) | ((gate == gn) & (lane < n)), 1.0, 0.0)
        sel_n = jnp.sum(beats, axis=1, keepdims=True) < MOBA_TOPK
        kb_t = jnp.concatenate([kp[ppb * n + r][...] for r in range(ppb)], 1).astype(BF16)
        s = jnp.where(sel_n, _nn(qab, kb_t), NEG)
        s_ref[:, n * MOBA_BLOCK:(n + 1) * MOBA_BLOCK] = s
        m = jnp.maximum(m, jnp.max(s, axis=1, keepdims=True))
    p_new = jnp.exp(s_new - m)
    l = jnp.sum(p_new, axis=1, keepdims=True)
    acc = _nn(p_new.astype(BF16), _pad_rows(vn_ref[...], LANES).astype(BF16))
    for n in range(nb):
        p = jnp.exp(s_ref[:, n * MOBA_BLOCK:(n + 1) * MOBA_BLOCK] - m)
        l = l + jnp.sum(p, axis=1, keepdims=True)
        vb_t = jnp.concatenate([vp[ppb * n + r][...] for r in range(ppb)], 1).astype(BF16)
        acc = acc + _nt(p.astype(BF16), vb_t)
    o_ref[...] = _unstack_decode(acc / l, s_len)


def _moba_sample(q, k_new, v_new, k_pool_t, v_pool_t, page_table, layer):
    b, s_len, _ = q.shape
    npages = page_table.shape[1]
    assert (npages * PAGE_SIZE) % MOBA_BLOCK == 0 and s_len <= LANES
    nb = npages * PAGE_SIZE // MOBA_BLOCK
    assert 0 < nb <= LANES

    def page_spec(p):
        return pl.BlockSpec((None, None, KVW, PAGE_SIZE), lambda i, pt: (layer, pt[i, p], 0, 0))

    tok = lambda i, pt: (i, 0, 0)
    grid_spec = pltpu.PrefetchScalarGridSpec(
        num_scalar_prefetch=1,
        grid=(b,),
        in_specs=[pl.BlockSpec((None, s_len, QW), tok), pl.BlockSpec((None, s_len, KVW), tok),
                  pl.BlockSpec((None, s_len, KVW), tok)]
                 + [page_spec(p) for p in range(npages)] + [page_spec(p) for p in range(npages)],
        out_specs=pl.BlockSpec((None, s_len, QW), tok),
        scratch_shapes=[pltpu.VMEM((N_HEADS * s_len, nb * MOBA_BLOCK), F32)],
    )
    return pl.pallas_call(
        functools.partial(_moba_sample_body, nb=nb, s_len=s_len),
        grid_spec=grid_spec,
        out_shape=jax.ShapeDtypeStruct((b, s_len, QW), F32),
        compiler_params=_params(("arbitrary",)),
        name="moba_sample",
    )(page_table, q, k_new, v_new, *([k_pool_t] * npages), *([v_pool_t] * npages))


def _swa_prompt_body(q_ref, kp_ref, kc_ref, vp_ref, vc_ref, sink_ref, o_ref):
    t = pl.program_id(1)
    cols = GROUP * WINDOW
    qt = _queries_t(q_ref[...] * SCALE)
    kk = jnp.concatenate([kp_ref[...], kc_ref[...]], 0).astype(BF16)
    vv_t = jnp.concatenate([vp_ref[...].T, vc_ref[...].T], 1).astype(BF16)
    kidx = lax.broadcasted_iota(jnp.int32, (2 * WINDOW, cols), 0)
    tq = lax.broadcasted_iota(jnp.int32, (2 * WINDOW, cols), 1) % WINDOW
    lo = jnp.where(t == 0, WINDOW, 0)
    valid = (kidx >= tq) & (kidx <= tq + WINDOW) & (kidx >= lo)
    outs = []
    for j in range(N_KV):
        s = jnp.where(valid, _nn(kk, _stack_heads_t(qt, j).astype(BF16)), NEG)
        sink = sink_ref[j]
        m = jnp.maximum(jnp.max(s, axis=0, keepdims=True), sink)
        p = jnp.exp(s - m)
        den = jnp.sum(p, axis=0, keepdims=True) + jnp.exp(sink - m)
        outs.append(_untranspose(_nn(vv_t, p.astype(BF16)) / den, WINDOW))
    o_ref[...] = _unstack_heads(outs, WINDOW)


def _swa_prompt(q, k, v, sink):
    b, l, _ = q.shape
    sink_cols = jnp.repeat(sink.reshape(N_KV, GROUP), WINDOW, axis=1)[:, None, :].astype(F32)
    cur = lambda i, t: (i, t, 0)
    prev = lambda i, t: (i, jnp.maximum(t - 1, 0), 0)
    return pl.pallas_call(
        _swa_prompt_body,
        grid=(b, l // WINDOW),
        in_specs=[pl.BlockSpec((None, WINDOW, QW), cur),
                  pl.BlockSpec((None, WINDOW, KVW), prev), pl.BlockSpec((None, WINDOW, KVW), cur),
                  pl.BlockSpec((None, WINDOW, KVW), prev), pl.BlockSpec((None, WINDOW, KVW), cur),
                  pl.BlockSpec((N_KV, 1, GROUP * WINDOW), lambda i, t: (0, 0, 0))],
        out_specs=pl.BlockSpec((None, WINDOW, QW), cur),
        out_shape=jax.ShapeDtypeStruct((b, l, QW), F32),
        compiler_params=_params(("parallel", "arbitrary")),
        name="swa_prompt",
    )(q, k, k, v, v, sink_cols)


def _swa_sample_body(q_ref, kn_ref, vn_ref, wk_ref, wv_ref, sink_ref, o_ref, nk_ref, nv_ref, *, bt, s_len):
    rows = N_HEADS * s_len
    tok = lax.broadcasted_iota(jnp.int32, (rows, LANES), 0) % s_len
    lane = lax.broadcasted_iota(jnp.int32, (rows, LANES), 1)
    lane_w = lax.broadcasted_iota(jnp.int32, (KVW, WINDOW), 1)
    sink = sink_ref[...]

    def one(i, carry):
        qab = _stack_decode(q_ref[i] * SCALE).astype(BF16)
        kn = _pad_rows(kn_ref[i], LANES)
        vn = _pad_rows(vn_ref[i], LANES)
        wk_t = wk_ref[i]
        wv_t = wv_ref[i]
        s_win = jnp.where(lane >= tok, _nn(qab, wk_t.astype(BF16)), NEG)
        s_new = jnp.where((lane <= tok) & (lane < s_len), _nt(qab, kn.astype(BF16)), NEG)
        m = jnp.maximum(jnp.maximum(jnp.max(s_win, axis=1, keepdims=True), jnp.max(s_new, axis=1, keepdims=True)), sink)
        p_win = jnp.exp(s_win - m)
        p_new = jnp.exp(s_new - m)
        den = jnp.sum(p_win, axis=1, keepdims=True) + jnp.sum(p_new, axis=1, keepdims=True) + jnp.exp(sink - m)
        acc = _nt(p_win.astype(BF16), wv_t.astype(BF16)) + _nn(p_new.astype(BF16), vn.astype(BF16))
        o_ref[i] = _unstack_decode(acc / den, s_len)
        keep = lane_w < WINDOW - s_len
        nk_ref[i] = jnp.where(keep, pltpu.roll(wk_t, WINDOW - s_len, 1), pltpu.roll(kn.T, WINDOW - s_len, 1))
        nv_ref[i] = jnp.where(keep, pltpu.roll(wv_t, WINDOW - s_len, 1), pltpu.roll(vn.T, WINDOW - s_len, 1))
        return carry

    lax.fori_loop(0, bt, one, 0)


def _swa_sample(q, k_new, v_new, win_k_t, win_v_t, sink, bt=16):
    b, s_len, _ = q.shape
    bt = min(bt, b)
    assert win_k_t.shape[2] == WINDOW == LANES and b % bt == 0 and s_len <= WINDOW
    sink_rows = jnp.repeat(sink.astype(F32), s_len)[:, None]
    blk = lambda i: (i, 0, 0)
    win = pl.BlockSpec((bt, KVW, WINDOW), blk)
    return pl.pallas_call(
        functools.partial(_swa_sample_body, bt=bt, s_len=s_len),
        grid=(b // bt,),
        in_specs=[pl.BlockSpec((bt, s_len, QW), blk), pl.BlockSpec((bt, s_len, KVW), blk),
                  pl.BlockSpec((bt, s_len, KVW), blk), win, win,
                  pl.BlockSpec((N_HEADS * s_len, 1), lambda i: (0, 0))],
        out_specs=[pl.BlockSpec((bt, s_len, QW), blk), win, win],
        out_shape=[jax.ShapeDtypeStruct((b, s_len, QW), F32), jax.ShapeDtypeStruct((b, KVW, WINDOW), F32),
                   jax.ShapeDtypeStruct((b, KVW, WINDOW), F32)],
        compiler_params=_params(("parallel",)),
        name="swa_sample",
    )(q, k_new, v_new, win_k_t, win_v_t, sink_rows)


def _conv_prompt_body(prev_ref, cur_ref, w_ref, b_ref, g_ref, bb_ref, o_ref, ext_ref, *, tm, halo):
    t = pl.program_id(1)
    ext_ref[0:halo, :] = jnp.where(t == 0, 0.0, prev_ref[...])
    ext_ref[halo:halo + tm, :] = cur_ref[...]
    kw = w_ref.shape[0]
    acc = jnp.zeros((tm, B_CH), F32) + b_ref[...]
    for j in range(kw):
        acc = acc + w_ref[j:j + 1, :] * ext_ref[pl.ds(halo - (kw - 1) + j, tm), :]
    o_ref[...] = _silu(_ln(acc, g_ref[...], bb_ref[...]))


def _conv_prompt(u, w, b, g, bb, tm=256, halo=32):
    bsz, l, _ = u.shape
    r = tm // halo
    vec = lambda i, t: (0, 0)
    return pl.pallas_call(
        functools.partial(_conv_prompt_body, tm=tm, halo=halo),
        grid=(bsz, l // tm),
        in_specs=[pl.BlockSpec((None, halo, B_CH), lambda i, t: (i, jnp.maximum(t * r - 1, 0), 0)),
                  pl.BlockSpec((None, tm, B_CH), lambda i, t: (i, t, 0)),
                  pl.BlockSpec(w.shape, vec), pl.BlockSpec((1, B_CH), vec), pl.BlockSpec((1, B_CH), vec),
                  pl.BlockSpec((1, B_CH), vec)],
        out_specs=pl.BlockSpec((None, tm, B_CH), lambda i, t: (i, t, 0)),
        out_shape=jax.ShapeDtypeStruct((bsz, l, B_CH), F32),
        scratch_shapes=[pltpu.VMEM((tm + halo, B_CH), F32)],
        compiler_params=_params(("parallel", "arbitrary")),
        name="conv_prompt",
    )(u, u, w, b[None], g[None], bb[None])


def _conv_sample_body(up_ref, w_ref, b_ref, g_ref, bb_ref, o_ref, *, bt, s_len):
    kw = w_ref.shape[0]
    acc = jnp.zeros((bt, s_len, B_CH), F32) + b_ref[...]
    for j in range(kw):
        acc = acc + w_ref[j:j + 1, :] * up_ref[:, pl.ds(j, s_len), :]
    o_ref[...] = _silu(_ln(acc, g_ref[...], bb_ref[...]))


def _conv_sample(up, w, b, g, bb, bt=32):
    bsz, rows, _ = up.shape
    bt = min(bt, bsz)
    s_len = rows - (w.shape[0] - 1)
    vec = lambda i: (0, 0)
    return pl.pallas_call(
        functools.partial(_conv_sample_body, bt=bt, s_len=s_len),
        grid=(bsz // bt,),
        in_specs=[pl.BlockSpec((bt, rows, B_CH), lambda i: (i, 0, 0)),
                  pl.BlockSpec(w.shape, vec), pl.BlockSpec((1, B_CH), vec), pl.BlockSpec((1, B_CH), vec),
                  pl.BlockSpec((1, B_CH), vec)],
        out_specs=pl.BlockSpec((bt, s_len, B_CH), lambda i: (i, 0, 0)),
        out_shape=jax.ShapeDtypeStruct((bsz, s_len, B_CH), F32),
        compiler_params=_params(("parallel",)),
        name="conv_sample",
    )(up, w, b[None], g[None], bb[None])


def _hgrn_tile(hq, df, v, lb_ref, ch):
    n = LANES // ch
    loglb, log1mlb, onemlb = lb_ref[0:1, :], lb_ref[1:2, :], lb_ref[2:3, :]
    e = jnp.exp(-jnp.abs(df))
    ls = jnp.minimum(df, 0.0) - jnp.log1p(e)
    b_ = log1mlb + ls
    mx = jnp.maximum(loglb, b_)
    logf = mx + jnp.log(jnp.exp(loglb - mx) + jnp.exp(b_ - mx))
    hk = onemlb * (jnp.where(df >= 0.0, e, 1.0) / (1.0 + e))
    row = lax.broadcasted_iota(jnp.int32, (LANES, LANES), 0)
    colv = lax.broadcasted_iota(jnp.int32, (LANES, LANES), 1)
    r = row % ch
    g = logf
    sh = 1
    while sh < ch:
        g = g + jnp.where(r >= sh, pltpu.roll(g, sh, 0), 0.0)
        sh *= 2
    qa, kb, kd, qe, dec = [], [], [], [], []
    for c in range(n):
        sl = slice(ch * c, ch * (c + 1))
        gc = g[sl]
        gm = gc[ch // 2 - 1:ch // 2]
        gl = gc[ch - 1:ch]
        qa.append(hq[sl] * jnp.exp(gc - gm))
        kb.append(hk[sl] * jnp.exp(gm - gc))
        kd.append(hk[sl] * jnp.exp(gl - gc))
        qe.append((hq[sl] * jnp.exp(gc)).astype(BF16))
        dec.append(jnp.exp(gl))
    att = _nt(jnp.concatenate(qa, 0).astype(BF16), jnp.concatenate(kb, 0).astype(BF16))
    att = jnp.where((row // ch == colv // ch) & (colv <= row), att, 0.0)
    vb = v.astype(BF16)
    o_intra = _nn(att.astype(BF16), vb)
    vt = v.T
    kdb = jnp.concatenate(kd, 0).astype(BF16)
    ut = [_nn(jnp.where(colv // ch == c, vt, 0.0).astype(BF16), kdb) for c in range(n)]
    return o_intra, ut, qe, dec


def _hgrn_finish(o, sg, gn):
    return o * lax.rsqrt(jnp.mean(o * o, axis=-1, keepdims=True) + LN_EPS) * gn * sg


def _hgrn_prompt_body(hq_ref, df_ref, di_ref, sg_ref, lb_ref, gn_ref, o_ref, s_ref, *, ngroups, ch):
    def group(gi, st):
        off = pl.multiple_of(gi * LANES, LANES)
        rows = pl.ds(off, LANES)
        o_intra, ut, qe, dec = _hgrn_tile(hq_ref[rows, :], df_ref[rows, :], di_ref[rows, :], lb_ref, ch)
        o_inter = []
        for c in range(LANES // ch):
            o_inter.append(_nt(qe[c], st.astype(BF16)))
            st = dec[c] * st + ut[c]
        o = o_intra + jnp.concatenate(o_inter, 0)
        o_ref[rows, :] = _hgrn_finish(o, sg_ref[rows, :], gn_ref[...])
        return st

    st = lax.fori_loop(0, ngroups, group, jnp.zeros((LANES, LANES), F32))
    s_ref[...] = st.T


def _hgrn_prompt(d4, lbt, gn, ch=16):
    b, l, _ = d4.shape
    seg = lambda s: pl.BlockSpec((None, l, LANES), lambda i, h: (i, 0, D_HEADS * s + h))
    return pl.pallas_call(
        functools.partial(_hgrn_prompt_body, ngroups=l // LANES, ch=ch),
        grid=(b, D_HEADS),
        in_specs=[seg(0), seg(1), seg(2), seg(3),
                  pl.BlockSpec((3, LANES), lambda i, h: (0, h)), pl.BlockSpec((1, LANES), lambda i, h: (0, 0))],
        out_specs=[pl.BlockSpec((None, l, LANES), lambda i, h: (i, 0, h)),
                   pl.BlockSpec((None, None, D_DK, LANES), lambda i, h: (i, h, 0, 0))],
        out_shape=[jax.ShapeDtypeStruct((b, l, D_HEADS * LANES), F32),
                   jax.ShapeDtypeStruct((b, D_HEADS, D_DK, LANES), F32)],
        compiler_params=_params(("parallel", "arbitrary")),
        name="hgrn_prompt",
    )(d4, d4, d4, d4, lbt, gn[None])


def _hgrn_sample_body(hq_ref, df_ref, di_ref, sg_ref, lb_ref, gn_ref, s0_ref, o_ref, s_ref, *, ch):
    o_intra, ut, qe, dec = _hgrn_tile(hq_ref[...], df_ref[...], di_ref[...], lb_ref, ch)
    o_inter = []
    for c in range(LANES // ch):
        st = s0_ref[c].T
        o_inter.append(_nt(qe[c], st.astype(BF16)))
        s_ref[c] = (dec[c] * st + ut[c]).T
    o = o_intra + jnp.concatenate(o_inter, 0)
    o_ref[...] = _hgrn_finish(o, sg_ref[...], gn_ref[...])


def _hgrn_sample(d4, lbt, gn, s0):
    m = d4.shape[0]
    b = s0.shape[0]
    s_len = m // b
    assert LANES % s_len == 0 and s_len & (s_len - 1) == 0 and m % LANES == 0
    bt = LANES // s_len
    seg = lambda s: pl.BlockSpec((LANES, LANES), lambda i, h: (i, D_HEADS * s + h))
    st_spec = pl.BlockSpec((bt, None, D_DK, LANES), lambda i, h: (i, h, 0, 0))
    return pl.pallas_call(
        functools.partial(_hgrn_sample_body, ch=s_len),
        grid=(m // LANES, D_HEADS),
        in_specs=[seg(0), seg(1), seg(2), seg(3),
                  pl.BlockSpec((3, LANES), lambda i, h: (0, h)), pl.BlockSpec((1, LANES), lambda i, h: (0, 0)), st_spec],
        out_specs=[pl.BlockSpec((LANES, LANES), lambda i, h: (i, h)), st_spec],
        out_shape=[jax.ShapeDtypeStruct((m, D_HEADS * LANES), F32), jax.ShapeDtypeStruct(s0.shape, F32)],
        compiler_params=_params(("parallel", "arbitrary")),
        name="hgrn_sample",
    )(d4, d4, d4, d4, lbt, gn[None], s0)


def _outproj_body(a_ref, c_ref, x_ref, w_ref, g_ref, b_ref, o_ref):
    wa = a_ref.shape[1]
    y = _nn(a_ref[...].astype(BF16), w_ref[0:wa, :]) + _nn(c_ref[...].astype(BF16), w_ref[wa:, :])
    o_ref[...] = _ln(ALPHA * x_ref[...] + y, g_ref[...], b_ref[...])


def _outproj_ln(a, c, x2, w_bf, g, b, tm):
    m = x2.shape[0]
    row = lambda i: (i, 0)
    vec = lambda i: (0, 0)
    return pl.pallas_call(
        _outproj_body,
        grid=(m // tm,),
        in_specs=[pl.BlockSpec((tm, a.shape[1]), row), pl.BlockSpec((tm, c.shape[1]), row),
                  pl.BlockSpec((tm, D_MODEL), row), pl.BlockSpec(w_bf.shape, vec),
                  pl.BlockSpec((1, D_MODEL), vec), pl.BlockSpec((1, D_MODEL), vec)],
        out_specs=pl.BlockSpec((tm, D_MODEL), row),
        out_shape=jax.ShapeDtypeStruct((m, D_MODEL), F32),
        compiler_params=_params(("parallel",)),
        name="outproj_ln",
    )(a, c, x2, w_bf, g[None], b[None])


def _ffn_body(x_ref, halo_ref, hist_ref, wg_ref, wu_ref, wd_ref, wc_ref, bc_ref, g_ref, b_ref,
              o_ref, st_ref, *, tm, seg, fc):
    x = x_ref[...]
    xb = x.astype(BF16)
    row = lax.broadcasted_iota(jnp.int32, (tm, fc), 0)
    r = row % seg
    acc = jnp.zeros((tm, D_MODEL), F32)
    for c in range(D_FF // fc):
        cols = slice(fc * c, fc * (c + 1))
        gp = _nn(xb, wg_ref[:, cols])
        if seg == tm:
            t = pl.program_id(1)
            hg = _nn(halo_ref[...].astype(BF16), wg_ref[:, cols])
            h0 = jnp.where(t == 0, hist_ref[0:1, cols], hg[6:7])
            h1 = jnp.where(t == 0, hist_ref[1:2, cols], hg[7:8])
            p1 = jnp.where(row == 0, h1, pltpu.roll(gp, 1, 0))
            p2 = jnp.where(row == 0, h0, jnp.where(row == 1, h1, pltpu.roll(gp, 2, 0)))
            st_ref[:, cols] = gp[tm - 2:tm]
        else:
            hp = hist_ref[:, cols]
            p1 = jnp.where(r == 0, pltpu.roll(hp, (tm - (seg - 1)) % tm, 0), pltpu.roll(gp, 1, 0))
            p2 = jnp.where(r < 2, pltpu.roll(hp, (tm - (seg - 2)) % tm, 0), pltpu.roll(gp, 2, 0))
            st_ref[:, cols] = gp
        cv = wc_ref[0:1, cols] * p2 + wc_ref[1:2, cols] * p1 + wc_ref[2:3, cols] * gp + bc_ref[:, cols]
        hid = 0.5 * cv * (1.0 + lax.erf(cv * (2.0 ** -0.5))) * _nn(xb, wu_ref[:, cols])
        acc = acc + _nn(hid.astype(BF16), wd_ref[cols, :])
    o_ref[...] = _ln(ALPHA * x + acc, g_ref[...], b_ref[...])


def _ffn_prompt(x, hist, wg, wu, wd, wc, bc, g, b, tm=512, fc=256):
    bsz, l, _ = x.shape
    vec = lambda i, t: (0, 0)
    r = tm // 8
    return pl.pallas_call(
        functools.partial(_ffn_body, tm=tm, seg=tm, fc=fc),
        grid=(bsz, l // tm),
        in_specs=[pl.BlockSpec((None, tm, D_MODEL), lambda i, t: (i, t, 0)),
                  pl.BlockSpec((None, 8, D_MODEL), lambda i, t: (i, jnp.maximum(t * r - 1, 0), 0)),
                  pl.BlockSpec((None, 2, D_FF), lambda i, t: (i, 0, 0)),
                  pl.BlockSpec(wg.shape, vec), pl.BlockSpec(wu.shape, vec), pl.BlockSpec(wd.shape, vec),
                  pl.BlockSpec(wc.shape, vec), pl.BlockSpec((1, D_FF), vec),
                  pl.BlockSpec((1, D_MODEL), vec), pl.BlockSpec((1, D_MODEL), vec)],
        out_specs=[pl.BlockSpec((None, tm, D_MODEL), lambda i, t: (i, t, 0)),
                   pl.BlockSpec((None, 2, D_FF), lambda i, t: (i, 0, 0))],
        out_shape=[jax.ShapeDtypeStruct((bsz, l, D_MODEL), F32), jax.ShapeDtypeStruct((bsz, 2, D_FF), F32)],
        compiler_params=_params(("parallel", "arbitrary")),
        name="ffn_prompt",
    )(x, x, hist, wg, wu, wd, wc, bc[None], g[None], b[None])


def _ffn_sample(x2, hist_rows, s_len, wg, wu, wd, wc, bc, g, b, tm=512, fc=256):
    m = x2.shape[0]
    tm = min(tm, m)
    vec = lambda i: (0, 0)
    row = lambda i: (i, 0)
    return pl.pallas_call(
        functools.partial(_ffn_body, tm=tm, seg=s_len, fc=fc),
        grid=(m // tm,),
        in_specs=[pl.BlockSpec((tm, D_MODEL), row), pl.BlockSpec((8, D_MODEL), vec), pl.BlockSpec((tm, D_FF), row),
                  pl.BlockSpec(wg.shape, vec), pl.BlockSpec(wu.shape, vec), pl.BlockSpec(wd.shape, vec),
                  pl.BlockSpec(wc.shape, vec), pl.BlockSpec((1, D_FF), vec),
                  pl.BlockSpec((1, D_MODEL), vec), pl.BlockSpec((1, D_MODEL), vec)],
        out_specs=[pl.BlockSpec((tm, D_MODEL), row), pl.BlockSpec((tm, D_FF), row)],
        out_shape=[jax.ShapeDtypeStruct((m, D_MODEL), F32), jax.ShapeDtypeStruct((m, D_FF), F32)],
        compiler_params=_params(("parallel",)),
        name="ffn_sample",
    )(x2, x2, hist_rows, wg, wu, wd, wc, bc[None], g[None], b[None])


def _lb_tables(lb_param):
    lbs = jnp.cumsum(jax.nn.softmax(lb_param.astype(F32), axis=0), axis=0)
    lbs = lbs - lbs[:1]
    return jnp.stack([jnp.log(lbs), jnp.log1p(-lbs), 1.0 - lbs], axis=1)


def _trunk_prompt(x, prm, lbt):
    b, l, _ = x.shape
    m = b * l
    tm = 512
    tabs = _rope_tables(jnp.arange(l))
    kv_k, kv_v, conv_b, win_k, win_v, hgrn, ffn = [], [], [], [], [], [], []
    zero_hist = jnp.zeros((b, FFN_CONV_WIDTH - 1, D_FF), F32)
    for layer in range(DEPTH):
        i = layer // 2
        x2 = x.reshape(m, D_MODEL)
        if layer % 2 == 0:
            q, k, v, u = _inproj(x2, prm["w_in_even"][i], tabs, True, tm)
            k3, v3, u3 = k.reshape(b, l, KVW), v.reshape(b, l, KVW), u.reshape(b, l, B_CH)
            a = _moba_prompt(q.reshape(b, l, QW), k3, v3)
            cv = _conv_prompt(u3, prm["w_dw_b"][i], prm["b_dw_b"][i], prm["conv_ln_g"][i], prm["conv_ln_b"][i])
            kv_k.append(k3.reshape(b, l, N_KV, HEAD_DIM))
            kv_v.append(v3.reshape(b, l, N_KV, HEAD_DIM))
            conv_b.append(u3[:, l - (B_CONV_WIDTH - 1):])
            x1 = _outproj_ln(a.reshape(m, QW), cv.reshape(m, B_CH), x2, prm["w_out_even"][i],
                             prm["ln1_g"][layer], prm["ln1_b"][layer], tm)
        else:
            q, k, v, d4 = _inproj(x2, prm["w_in_odd"][i], tabs, False, tm)
            k3, v3 = k.reshape(b, l, KVW), v.reshape(b, l, KVW)
            a = _swa_prompt(q.reshape(b, l, QW), k3, v3, prm["sinks"][i])
            o, st = _hgrn_prompt(d4.reshape(b, l, 2048), lbt[i], prm["gnorm_g"][i])
            keep = min(WINDOW, l)
            win_k.append(k3[:, l - keep:].reshape(b, keep, N_KV, HEAD_DIM))
            win_v.append(v3[:, l - keep:].reshape(b, keep, N_KV, HEAD_DIM))
            hgrn.append(st)
            x1 = _outproj_ln(a.reshape(m, QW), o.reshape(m, 512), x2, prm["w_out_odd"][i],
                             prm["ln1_g"][layer], prm["ln1_b"][layer], tm)
        xo, fb = _ffn_prompt(x1.reshape(b, l, D_MODEL), zero_hist, prm["w_ffn_gate"][layer], prm["w_ffn_up"][layer],
                             prm["w_ffn_down"][layer], prm["w_ffn_conv"][layer], prm["b_ffn_conv"][layer],
                             prm["ln2_g"][layer], prm["ln2_b"][layer])
        ffn.append(fb)
        x = xo
    return (x, jnp.stack(kv_k), jnp.stack(kv_v), jnp.stack(conv_b), jnp.stack(win_k), jnp.stack(win_v),
            jnp.stack(hgrn), jnp.stack(ffn))


def _trunk_sample(x, prm, lbt, cache_k, cache_v, page_table, state_conv_b, state_win_k, state_win_v,
                  state_hgrn, state_ffn):
    b, s, _ = x.shape
    m = b * s
    tm = min(512, m)
    past = page_table.shape[1] * PAGE_SIZE
    assert state_win_k.shape[2] == WINDOW and past + s >= WINDOW
    tabs = _rope_tables(jnp.tile(past + jnp.arange(s), tm // s))
    pool_k_t = _kv_transposed(cache_k)
    pool_v_t = _kv_transposed(cache_v)
    win_k_t = _kv_transposed(state_win_k)
    win_v_t = _kv_transposed(state_win_v)
    kv_k, kv_v, conv_b, win_k, win_v, hgrn, ffn = [], [], [], [], [], [], []
    for layer in range(DEPTH):
        i = layer // 2
        x2 = x.reshape(m, D_MODEL)
        if layer % 2 == 0:
            q, k, v, u = _inproj(x2, prm["w_in_even"][i], tabs, True, tm)
            k3, v3 = k.reshape(b, s, KVW), v.reshape(b, s, KVW)
            a = _moba_sample(q.reshape(b, s, QW), k3, v3, pool_k_t, pool_v_t, page_table, i)
            up = jnp.concatenate([state_conv_b[i], u.reshape(b, s, B_CH)], 1)
            cv = _conv_sample(up, prm["w_dw_b"][i], prm["b_dw_b"][i], prm["conv_ln_g"][i], prm["conv_ln_b"][i])
            kv_k.append(k3.reshape(b, s, N_KV, HEAD_DIM))
            kv_v.append(v3.reshape(b, s, N_KV, HEAD_DIM))
            conv_b.append(up[:, up.shape[1] - (B_CONV_WIDTH - 1):])
            x1 = _outproj_ln(a.reshape(m, QW), cv.reshape(m, B_CH), x2, prm["w_out_even"][i],
                             prm["ln1_g"][layer], prm["ln1_b"][layer], tm)
        else:
            q, k, v, d4 = _inproj(x2, prm["w_in_odd"][i], tabs, False, tm)
            k3, v3 = k.reshape(b, s, KVW), v.reshape(b, s, KVW)
            a, nk_t, nv_t = _swa_sample(q.reshape(b, s, QW), k3, v3, win_k_t[i], win_v_t[i], prm["sinks"][i])
            o, st = _hgrn_sample(d4, lbt[i], prm["gnorm_g"][i], state_hgrn[i])
            win_k.append(_kv_untransposed(nk_t))
            win_v.append(_kv_untransposed(nv_t))
            hgrn.append(st)
            x1 = _outproj_ln(a.reshape(m, QW), o, x2, prm["w_out_odd"][i],
                             prm["ln1_g"][layer], prm["ln1_b"][layer], tm)
        hist_rows = jnp.pad(state_ffn[layer], ((0, 0), (s - (FFN_CONV_WIDTH - 1), 0), (0, 0))).reshape(m, D_FF)
        xo, gp = _ffn_sample(x1, hist_rows, s, prm["w_ffn_gate"][layer], prm["w_ffn_up"][layer],
                             prm["w_ffn_down"][layer], prm["w_ffn_conv"][layer], prm["b_ffn_conv"][layer],
                             prm["ln2_g"][layer], prm["ln2_b"][layer], tm=tm)
        ffn.append(gp.reshape(b, s, D_FF)[:, s - (FFN_CONV_WIDTH - 1):])
        x = xo.reshape(b, s, D_MODEL)
    return (x, jnp.stack(kv_k), jnp.stack(kv_v), jnp.stack(conv_b), jnp.stack(win_k), jnp.stack(win_v),
            jnp.stack(hgrn), jnp.stack(ffn))


def kernel(x_prompt, x_sample, cache_k, cache_v, state_conv_b, state_win_k, state_win_v, state_hgrn, state_ffn,
           page_table, w_in_even, w_dw_b, b_dw_b, conv_ln_g, conv_ln_b, w_out_even, w_in_odd, sinks, lb_param,
           gnorm_g, w_out_odd, ln1_g, ln1_b, ln2_g, ln2_b, w_ffn_gate, w_ffn_up, w_ffn_conv, b_ffn_conv, w_ffn_down):
    prm = dict(w_in_even=w_in_even.astype(BF16), w_dw_b=w_dw_b, b_dw_b=b_dw_b, conv_ln_g=conv_ln_g,
               conv_ln_b=conv_ln_b, w_out_even=w_out_even.astype(BF16), w_in_odd=w_in_odd.astype(BF16), sinks=sinks,
               gnorm_g=gnorm_g, w_out_odd=w_out_odd.astype(BF16), ln1_g=ln1_g, ln1_b=ln1_b, ln2_g=ln2_g, ln2_b=ln2_b,
               w_ffn_gate=w_ffn_gate.astype(BF16), w_ffn_up=w_ffn_up.astype(BF16), w_ffn_conv=w_ffn_conv,
               b_ffn_conv=b_ffn_conv, w_ffn_down=w_ffn_down.astype(BF16))
    lbt = _lb_tables(lb_param)
    yp, kkp, kvp, cbp, wkp, wvp, hgp, ffp = _trunk_prompt(x_prompt, prm, lbt)
    ys, kks, kvs, cbs, wks, wvs, hgs, ffs = _trunk_sample(
        x_sample, prm, lbt, cache_k, cache_v, page_table, state_conv_b, state_win_k, state_win_v, state_hgrn, state_ffn)
    return (yp, ys, kkp, kvp, kks, kvs, cbp, cbs, wkp, wvp, wks, wvs, hgp, hgs, ffp, ffs)
```

```python
import functools

import jax
import jax.numpy as jnp
from jax import lax
from jax.experimental import pallas as pl
from jax.experimental.pallas import tpu as pltpu

F32 = jnp.float32
BF16 = jnp.bfloat16

D_MODEL = 1024
DEPTH = 4
N_PAIR = DEPTH // 2
HEAD_DIM = 64
ROT_DIM = HEAD_DIM // 4
ROPE_THETA = 500000.0
N_HEADS = 8
N_KV = 2
GROUP = N_HEADS // N_KV
QW = N_HEADS * HEAD_DIM
KVW = N_KV * HEAD_DIM
MOBA_BLOCK = 256
MOBA_TOPK = 3
Q_CHUNK = 128
PAGE_SIZE = 128
B_CH = 512
B_CONV_WIDTH = 31
WINDOW = 128
D_HEADS = 4
D_DK = 128
D_FF = 2816
FFN_CONV_WIDTH = 3
LN_EPS = 1e-5
ALPHA = (2.0 * DEPTH) ** 0.25
SCALE = HEAD_DIM ** -0.5
EVEN_IN = QW + 2 * KVW + 2 * B_CH
ODD_IN = QW + 2 * KVW + 4 * 512

LANES = 128
NEG = -1e30
VMEM_LIMIT = 56 * 1024 * 1024

HIGHEST = lax.Precision.HIGHEST


def _nn(a, b, precision=None):
    return lax.dot_general(a, b, (((1,), (0,)), ((), ())), preferred_element_type=F32, precision=precision)


def _nt(a, b, precision=None):
    return lax.dot_general(a, b, (((1,), (1,)), ((), ())), preferred_element_type=F32, precision=precision)


def _ln(z, g, b):
    mu = jnp.mean(z, axis=-1, keepdims=True)
    d = z - mu
    var = jnp.mean(d * d, axis=-1, keepdims=True)
    return d * lax.rsqrt(var + LN_EPS) * g + b


def _silu(x):
    return x * jax.nn.sigmoid(x)


def _params(sem):
    return pltpu.CompilerParams(dimension_semantics=sem, vmem_limit_bytes=VMEM_LIMIT)


def _rope_tables(pos):
    half = ROT_DIM // 2
    inv = ROPE_THETA ** (-jnp.arange(half, dtype=F32) * 2.0 / ROT_DIM)
    ang = pos.astype(F32)[:, None] * inv[None, :]
    cos, sin = jnp.cos(ang), jnp.sin(ang)
    p = pos.shape[0]
    one = jnp.ones((p, HEAD_DIM - ROT_DIM), F32)
    zero = jnp.zeros((p, HEAD_DIM - ROT_DIM), F32)
    zh = jnp.zeros((p, half), F32)
    c = jnp.concatenate([cos, cos, one], 1)
    sa = jnp.concatenate([-sin, zh, zero], 1)
    sb = jnp.concatenate([zh, sin, zero], 1)
    return tuple(jnp.tile(t, (1, LANES // HEAD_DIM)) for t in (c, sa, sb))


def _rope(z, c, sa, sb):
    return z * c + pltpu.roll(z, LANES - ROT_DIM // 2, 1) * sa + pltpu.roll(z, ROT_DIM // 2, 1) * sb


def _inproj_even_body(x_ref, w_ref, c_ref, sa_ref, sb_ref, q_ref, k_ref, v_ref, u_ref):
    xb = x_ref[...].astype(BF16)
    c, sa, sb = c_ref[...], sa_ref[...], sb_ref[...]
    hq = _nn(xb, w_ref[:, 0:QW])
    for s in range(QW // LANES):
        q_ref[:, LANES * s:LANES * (s + 1)] = _rope(hq[:, LANES * s:LANES * (s + 1)], c, sa, sb)
    hkv = _nn(xb, w_ref[:, QW:QW + 2 * KVW])
    k_ref[...] = _rope(hkv[:, 0:KVW], c, sa, sb)
    v_ref[...] = hkv[:, KVW:2 * KVW]
    o = QW + 2 * KVW
    ga = _nn(xb, w_ref[:, o:o + B_CH])
    gb = _nn(xb, w_ref[:, o + B_CH:o + 2 * B_CH])
    u_ref[...] = ga * jax.nn.sigmoid(gb)


def _inproj_odd_body(x_ref, w_ref, c_ref, sa_ref, sb_ref, q_ref, k_ref, v_ref, d_ref):
    xb = x_ref[...].astype(BF16)
    c, sa, sb = c_ref[...], sa_ref[...], sb_ref[...]
    hq = _nn(xb, w_ref[:, 0:QW])
    for s in range(QW // LANES):
        q_ref[:, LANES * s:LANES * (s + 1)] = _rope(hq[:, LANES * s:LANES * (s + 1)], c, sa, sb)
    hkv = _nn(xb, w_ref[:, QW:QW + 2 * KVW])
    k_ref[...] = _rope(hkv[:, 0:KVW], c, sa, sb)
    v_ref[...] = hkv[:, KVW:2 * KVW]
    o = QW + 2 * KVW
    d_ref[:, 0:512] = _silu(_nn(xb, w_ref[:, o:o + 512]))
    d_ref[:, 512:1536] = _nn(xb, w_ref[:, o + 512:o + 1536])
    d_ref[:, 1536:2048] = _silu(_nn(xb, w_ref[:, o + 1536:o + 2048]))


def _inproj(x2, w_bf, tabs, even, tm):
    m = x2.shape[0]
    n_in = w_bf.shape[1]
    ntab = tabs[0].shape[0] // tm
    row = lambda i: (i, 0)
    tab = lambda i: (i % ntab, 0)
    wide = B_CH if even else 2048
    return pl.pallas_call(
        _inproj_even_body if even else _inproj_odd_body,
        grid=(m // tm,),
        in_specs=[pl.BlockSpec((tm, D_MODEL), row), pl.BlockSpec((D_MODEL, n_in), lambda i: (0, 0)),
                  pl.BlockSpec((tm, LANES), tab), pl.BlockSpec((tm, LANES), tab), pl.BlockSpec((tm, LANES), tab)],
        out_specs=[pl.BlockSpec((tm, QW), row), pl.BlockSpec((tm, KVW), row), pl.BlockSpec((tm, KVW), row),
                   pl.BlockSpec((tm, wide), row)],
        out_shape=[jax.ShapeDtypeStruct((m, QW), F32), jax.ShapeDtypeStruct((m, KVW), F32),
                   jax.ShapeDtypeStruct((m, KVW), F32), jax.ShapeDtypeStruct((m, wide), F32)],
        compiler_params=_params(("parallel",)),
        name="inproj_even" if even else "inproj_odd",
    )(x2, w_bf, *tabs)


def _queries_t(q):
    return jnp.concatenate([q[:, LANES * s:LANES * (s + 1)].T for s in range(QW // LANES)], 0)


def _stack_heads_t(qt, j):
    t = qt.shape[1]
    z = jnp.zeros((HEAD_DIM, t), F32)
    blocks = []
    for g in range(GROUP):
        h = GROUP * j + g
        hd = qt[HEAD_DIM * h:HEAD_DIM * (h + 1), :]
        blocks.append(jnp.concatenate([hd, z] if j == 0 else [z, hd], 0))
    return jnp.concatenate(blocks, 1)


def _unstack_heads(outs, t):
    lane = lax.broadcasted_iota(jnp.int32, (t, LANES), 1)
    slabs = []
    for p in range(N_HEADS // 2):
        halves = []
        for hh in range(2):
            h = 2 * p + hh
            j, g = h // GROUP, h % GROUP
            blk = outs[j][g * t:(g + 1) * t]
            if j != hh:
                blk = pltpu.roll(blk, HEAD_DIM, 1)
            halves.append(blk)
        slabs.append(jnp.where(lane < HEAD_DIM, halves[0], halves[1]))
    return jnp.concatenate(slabs, 1)


def _untranspose(o_t, t):
    return jnp.concatenate([o_t[:, t * g:t * (g + 1)].T for g in range(GROUP)], 0)


def _moba_prompt_body(q_ref, k_ref, v_ref, o_ref, km_ref, vt_ref, sel_ref, *, nb):
    c = pl.program_id(1)
    blk = c // (MOBA_BLOCK // Q_CHUNK)
    qs0 = c * Q_CHUNK
    cols = N_HEADS * Q_CHUNK
    nbp = km_ref.shape[0]

    @pl.when(c == 0)
    def _():
        km_ref[...] = jnp.concatenate(
            [jnp.mean(k_ref[n * MOBA_BLOCK:(n + 1) * MOBA_BLOCK, :], axis=0, keepdims=True) for n in range(nb)]
            + [jnp.zeros((nbp - nb, LANES), F32)] * (nbp > nb), 0)
        for n in range(nb):
            vt_ref[n] = v_ref[n * MOBA_BLOCK:(n + 1) * MOBA_BLOCK, :].T.astype(BF16)

    qt = _queries_t(q_ref[...] * SCALE)
    blk_row = lax.broadcasted_iota(jnp.int32, (nbp, cols), 0)
    kidx = lax.broadcasted_iota(jnp.int32, (MOBA_BLOCK, cols), 0)
    qpos = qs0 + lax.broadcasted_iota(jnp.int32, (MOBA_BLOCK, cols), 1) % Q_CHUNK
    qs_t = jnp.concatenate([_stack_heads_t(qt, j) for j in range(N_KV)], 1)
    qsb = qs_t.astype(BF16)
    gate = jnp.where(blk_row < blk, _nn(km_ref[...], qs_t, HIGHEST), -jnp.inf)
    sel_rows = []
    for n in range(nb):
        gn = gate[n:n + 1, :]
        beats = jnp.where((gate > gn) | ((gate == gn) & (blk_row < n)), 1.0, 0.0)
        cnt = jnp.sum(beats, axis=0, keepdims=True)
        sel_rows.append(jnp.where((cnt < MOBA_TOPK) & (n < blk), 1.0, 0.0))
    sel_ref[...] = jnp.concatenate(sel_rows + [jnp.zeros((nbp - nb, cols), F32)] * (nbp > nb), 0)

    def scores(n):
        off = pl.multiple_of(n * MOBA_BLOCK, MOBA_BLOCK)
        return _nn(k_ref[pl.ds(off, MOBA_BLOCK), :].astype(BF16), qsb), vt_ref[n]

    s, vt = scores(blk)
    s = jnp.where(kidx + blk * MOBA_BLOCK <= qpos, s, NEG)
    m = jnp.max(s, axis=0, keepdims=True)
    p = jnp.exp(s - m)
    l = jnp.sum(p, axis=0, keepdims=True)
    acc = _nn(vt, p.astype(BF16))

    def past(n, carry):
        m, l, acc = carry
        s, vt = scores(n)
        s = jnp.where(sel_ref[pl.ds(n, 1), :] > 0.5, s, NEG)
        m_new = jnp.maximum(m, jnp.max(s, axis=0, keepdims=True))
        a = jnp.exp(m - m_new)
        p = jnp.exp(s - m_new)
        return m_new, a * l + jnp.sum(p, axis=0, keepdims=True), a * acc + _nn(vt, p.astype(BF16))

    m, l, acc = lax.fori_loop(0, blk, past, (m, l, acc))
    o_t = acc / l
    half = GROUP * Q_CHUNK
    o_ref[...] = _unstack_heads([_untranspose(o_t[:, half * j:half * (j + 1)], Q_CHUNK) for j in range(N_KV)], Q_CHUNK)


def _moba_prompt(q, k, v):
    b, l, _ = q.shape
    nb = l // MOBA_BLOCK
    nbp = -(-nb // 8) * 8
    return pl.pallas_call(
        functools.partial(_moba_prompt_body, nb=nb),
        grid=(b, l // Q_CHUNK),
        in_specs=[pl.BlockSpec((None, Q_CHUNK, QW), lambda i, c: (i, c, 0)),
                  pl.BlockSpec((None, l, KVW), lambda i, c: (i, 0, 0)),
                  pl.BlockSpec((None, l, KVW), lambda i, c: (i, 0, 0))],
        out_specs=pl.BlockSpec((None, Q_CHUNK, QW), lambda i, c: (i, c, 0)),
        out_shape=jax.ShapeDtypeStruct((b, l, QW), F32),
        scratch_shapes=[pltpu.VMEM((nbp, LANES), F32), pltpu.VMEM((nb, KVW, MOBA_BLOCK), BF16),
                        pltpu.VMEM((nbp, N_HEADS * Q_CHUNK), F32)],
        compiler_params=_params(("parallel", "arbitrary")),
        name="moba_prompt",
    )(q, k, v)


def _stack_decode(q):
    s = q.shape[0]
    lane = lax.broadcasted_iota(jnp.int32, (s, LANES), 1)
    parts = []
    for h in range(N_HEADS):
        j = h // GROUP
        slab = q[:, LANES * (h // 2):LANES * (h // 2 + 1)]
        if h % 2 != j:
            slab = pltpu.roll(slab, HEAD_DIM, 1)
        keep = (lane < HEAD_DIM) if j == 0 else (lane >= HEAD_DIM)
        parts.append(jnp.where(keep, slab, 0.0))
    return jnp.concatenate(parts, 0)


def _unstack_decode(o, s):
    lane = lax.broadcasted_iota(jnp.int32, (s, LANES), 1)
    slabs = []
    for p in range(N_HEADS // 2):
        halves = []
        for hh in range(2):
            h = 2 * p + hh
            blk = o[h * s:(h + 1) * s]
            if h // GROUP != hh:
                blk = pltpu.roll(blk, HEAD_DIM, 1)
            halves.append(blk)
        slabs.append(jnp.where(lane < HEAD_DIM, halves[0], halves[1]))
    return jnp.concatenate(slabs, 1)


def _pad_rows(x, n):
    return jnp.concatenate([x, jnp.zeros((n - x.shape[0], x.shape[1]), x.dtype)], 0)


def _kv_transposed(x):
    lead = x.shape[:-3]
    n = len(lead)
    t = x.shape[-3]
    return jnp.transpose(x, tuple(range(n)) + (n + 1, n + 2, n)).reshape(lead + (KVW, t))


def _kv_untransposed(xt):
    lead = xt.shape[:-2]
    n = len(lead)
    t = xt.shape[-1]
    return jnp.transpose(xt.reshape(lead + (N_KV, HEAD_DIM, t)), tuple(range(n)) + (n + 2, n, n + 1))


def _moba_sample_body(pt_ref, q_ref, kn_ref, vn_ref, *rest, nb, s_len):
    ppb = MOBA_BLOCK // PAGE_SIZE
    npages = nb * ppb
    kp, vp = rest[:npages], rest[npages:2 * npages]
    o_ref, kb_ref, s_ref = rest[2 * npages:2 * npages + 3]
    rows = N_HEADS * s_len
    nbp = -(-nb // 8) * 8
    qt = _pad_rows(_stack_decode(q_ref[...] * SCALE), LANES).T
    qtb = qt.astype(BF16)
    sums = []
    for p in range(npages):
        kpg = kp[p][...].T
        kb_ref[p] = kpg.astype(BF16)
        sums.append(jnp.sum(kpg, axis=0, keepdims=True))
    km = jnp.concatenate([sum(sums[ppb * n:ppb * (n + 1)]) * (1.0 / MOBA_BLOCK) for n in range(nb)]
                         + [jnp.zeros((nbp - nb, LANES), F32)] * (nbp > nb), 0)
    blk_row = lax.broadcasted_iota(jnp.int32, (nbp, LANES), 0)
    gate = jnp.where(blk_row < nb, _nn(km, qt, HIGHEST), -jnp.inf)
    sel = []
    for n in range(nb):
        gn = gate[n:n + 1, :]
        beats = jnp.where((gate > gn) | ((gate == gn) & (blk_row < n)), 1.0, 0.0)
        sel.append(jnp.sum(beats, axis=0, keepdims=True) < MOBA_TOPK)
    kidx = lax.broadcasted_iota(jnp.int32, (LANES, LANES), 0)
    tok = lax.broadcasted_iota(jnp.int32, (LANES, LANES), 1) % s_len
    s_new = jnp.where((kidx <= tok) & (kidx < s_len), _nn(_pad_rows(kn_ref[...], LANES).astype(BF16), qtb), NEG)
    m = jnp.max(s_new, axis=0, keepdims=True)
    for p in range(npages):
        s = jnp.where(sel[p // ppb], _nn(kb_ref[p], qtb), NEG)
        s_ref[p] = s
        m = jnp.maximum(m, jnp.max(s, axis=0, keepdims=True))
    p_new = jnp.exp(s_new - m)
    l = jnp.sum(p_new, axis=0, keepdims=True)
    acc = _nn(_pad_rows(vn_ref[...], LANES).T.astype(BF16), p_new.astype(BF16))
    for p in range(npages):
        pp = jnp.exp(s_ref[p] - m)
        l = l + jnp.sum(pp, axis=0, keepdims=True)
        acc = acc + _nn(vp[p][...].astype(BF16), pp.astype(BF16))
    o_ref[...] = _unstack_decode((acc / l).T[0:rows], s_len)


def _moba_sample(q, k_new, v_new, k_pool_t, v_pool_t, page_table, layer):
    b, s_len, _ = q.shape
    npages = page_table.shape[1]
    assert (npages * PAGE_SIZE) % MOBA_BLOCK == 0 and N_HEADS * s_len <= LANES
    nb = npages * PAGE_SIZE // MOBA_BLOCK
    assert 0 < nb <= LANES

    def page_spec(p):
        return pl.BlockSpec((None, None, KVW, PAGE_SIZE), lambda i, pt: (layer, pt[i, p], 0, 0))

    tok = lambda i, pt: (i, 0, 0)
    grid_spec = pltpu.PrefetchScalarGridSpec(
        num_scalar_prefetch=1,
        grid=(b,),
        in_specs=[pl.BlockSpec((None, s_len, QW), tok), pl.BlockSpec((None, s_len, KVW), tok),
                  pl.BlockSpec((None, s_len, KVW), tok)]
                 + [page_spec(p) for p in range(npages)] + [page_spec(p) for p in range(npages)],
        out_specs=pl.BlockSpec((None, s_len, QW), tok),
        scratch_shapes=[pltpu.VMEM((npages, PAGE_SIZE, KVW), BF16), pltpu.VMEM((npages, PAGE_SIZE, LANES), F32)],
    )
    return pl.pallas_call(
        functools.partial(_moba_sample_body, nb=nb, s_len=s_len),
        grid_spec=grid_spec,
        out_shape=jax.ShapeDtypeStruct((b, s_len, QW), F32),
        compiler_params=_params(("arbitrary",)),
        name="moba_sample",
    )(page_table, q, k_new, v_new, *([k_pool_t] * npages), *([v_pool_t] * npages))


def _swa_prompt_body(q_ref, kp_ref, kc_ref, vp_ref, vc_ref, sink_ref, o_ref):
    t = pl.program_id(1)
    cols = N_HEADS * WINDOW
    qt = _queries_t(q_ref[...] * SCALE)
    kk = jnp.concatenate([kp_ref[...], kc_ref[...]], 0).astype(BF16)
    vv_t = jnp.concatenate([vp_ref[...].T, vc_ref[...].T], 1).astype(BF16)
    kidx = lax.broadcasted_iota(jnp.int32, (2 * WINDOW, cols), 0)
    tq = lax.broadcasted_iota(jnp.int32, (2 * WINDOW, cols), 1) % WINDOW
    lo = jnp.where(t == 0, WINDOW, 0)
    valid = (kidx >= tq) & (kidx <= tq + WINDOW) & (kidx >= lo)
    qs_t = jnp.concatenate([_stack_heads_t(qt, j) for j in range(N_KV)], 1)
    s = jnp.where(valid, _nn(kk, qs_t.astype(BF16)), NEG)
    sink = sink_ref[...]
    m = jnp.maximum(jnp.max(s, axis=0, keepdims=True), sink)
    p = jnp.exp(s - m)
    den = jnp.sum(p, axis=0, keepdims=True) + jnp.exp(sink - m)
    o_t = _nn(vv_t, p.astype(BF16)) / den
    half = GROUP * WINDOW
    o_ref[...] = _unstack_heads([_untranspose(o_t[:, half * j:half * (j + 1)], WINDOW) for j in range(N_KV)], WINDOW)


def _swa_prompt(q, k, v, sink):
    b, l, _ = q.shape
    sink_cols = jnp.repeat(sink.astype(F32), WINDOW)[None, :]
    cur = lambda i, t: (i, t, 0)
    prev = lambda i, t: (i, jnp.maximum(t - 1, 0), 0)
    return pl.pallas_call(
        _swa_prompt_body,
        grid=(b, l // WINDOW),
        in_specs=[pl.BlockSpec((None, WINDOW, QW), cur),
                  pl.BlockSpec((None, WINDOW, KVW), prev), pl.BlockSpec((None, WINDOW, KVW), cur),
                  pl.BlockSpec((None, WINDOW, KVW), prev), pl.BlockSpec((None, WINDOW, KVW), cur),
                  pl.BlockSpec((1, N_HEADS * WINDOW), lambda i, t: (0, 0))],
        out_specs=pl.BlockSpec((None, WINDOW, QW), cur),
        out_shape=jax.ShapeDtypeStruct((b, l, QW), F32),
        compiler_params=_params(("parallel", "arbitrary")),
        name="swa_prompt",
    )(q, k, k, v, v, sink_cols)


def _swa_sample_body(q_ref, kn_ref, vn_ref, wk_ref, wv_ref, sink_ref, o_ref, nk_ref, nv_ref, *, bt, s_len):
    rows = N_HEADS * s_len
    tok = lax.broadcasted_iota(jnp.int32, (rows, LANES), 0) % s_len
    lane = lax.broadcasted_iota(jnp.int32, (rows, LANES), 1)
    lane_w = lax.broadcasted_iota(jnp.int32, (KVW, WINDOW), 1)
    sink = sink_ref[...]

    def one(i, carry):
        qab = _stack_decode(q_ref[i] * SCALE).astype(BF16)
        kn = _pad_rows(kn_ref[i], LANES)
        vn = _pad_rows(vn_ref[i], LANES)
        wk_t = wk_ref[i]
        wv_t = wv_ref[i]
        s_win = jnp.where(lane >= tok, _nn(qab, wk_t.astype(BF16)), NEG)
        s_new = jnp.where((lane <= tok) & (lane < s_len), _nt(qab, kn.astype(BF16)), NEG)
        m = jnp.maximum(jnp.maximum(jnp.max(s_win, axis=1, keepdims=True), jnp.max(s_new, axis=1, keepdims=True)), sink)
        p_win = jnp.exp(s_win - m)
        p_new = jnp.exp(s_new - m)
        den = jnp.sum(p_win, axis=1, keepdims=True) + jnp.sum(p_new, axis=1, keepdims=True) + jnp.exp(sink - m)
        acc = _nt(p_win.astype(BF16), wv_t.astype(BF16)) + _nn(p_new.astype(BF16), vn.astype(BF16))
        o_ref[i] = _unstack_decode(acc / den, s_len)
        keep = lane_w < WINDOW - s_len
        nk_ref[i] = jnp.where(keep, pltpu.roll(wk_t, WINDOW - s_len, 1), pltpu.roll(kn.T, WINDOW - s_len, 1))
        nv_ref[i] = jnp.where(keep, pltpu.roll(wv_t, WINDOW - s_len, 1), pltpu.roll(vn.T, WINDOW - s_len, 1))
        return carry

    lax.fori_loop(0, bt, one, 0)


def _swa_sample(q, k_new, v_new, win_k_t, win_v_t, sink, bt=16):
    b, s_len, _ = q.shape
    bt = min(bt, b)
    assert win_k_t.shape[2] == WINDOW == LANES and b % bt == 0 and s_len <= WINDOW
    sink_rows = jnp.repeat(sink.astype(F32), s_len)[:, None]
    blk = lambda i: (i, 0, 0)
    win = pl.BlockSpec((bt, KVW, WINDOW), blk)
    return pl.pallas_call(
        functools.partial(_swa_sample_body, bt=bt, s_len=s_len),
        grid=(b // bt,),
        in_specs=[pl.BlockSpec((bt, s_len, QW), blk), pl.BlockSpec((bt, s_len, KVW), blk),
                  pl.BlockSpec((bt, s_len, KVW), blk), win, win,
                  pl.BlockSpec((N_HEADS * s_len, 1), lambda i: (0, 0))],
        out_specs=[pl.BlockSpec((bt, s_len, QW), blk), win, win],
        out_shape=[jax.ShapeDtypeStruct((b, s_len, QW), F32), jax.ShapeDtypeStruct((b, KVW, WINDOW), F32),
                   jax.ShapeDtypeStruct((b, KVW, WINDOW), F32)],
        compiler_params=_params(("parallel",)),
        name="swa_sample",
    )(q, k_new, v_new, win_k_t, win_v_t, sink_rows)


def _conv_prompt_body(prev_ref, cur_ref, w_ref, b_ref, g_ref, bb_ref, o_ref, ext_ref, sh_ref, *, tm, halo):
    t = pl.program_id(1)
    ext_ref[0:halo, :] = jnp.where(t == 0, 0.0, prev_ref[...])
    ext_ref[halo:halo + tm, :] = cur_ref[...]
    kw = w_ref.shape[0]
    base = halo - (kw - 1)
    sub = 8
    for lc in range(B_CH // LANES):
        ln = slice(LANES * lc, LANES * (lc + 1))
        acc = jnp.zeros((tm, LANES), F32) + b_ref[:, ln]
        for r in range(sub):
            offs = [o for o in range(base, base + kw) if o % sub == r]
            if not offs:
                continue
            n = tm + max(offs) - r
            if r:
                sh_ref[0:n, :] = ext_ref[pl.ds(r, n), ln]
            src = sh_ref if r else ext_ref.at[:, ln]
            for o in offs:
                acc = acc + w_ref[o - base:o - base + 1, ln] * src[pl.ds(o - r, tm), :]
        o_ref[:, ln] = acc
    o_ref[...] = _silu(_ln(o_ref[...], g_ref[...], bb_ref[...]))


def _conv_prompt(u, w, b, g, bb, tm=256, halo=32):
    bsz, l, _ = u.shape
    r = tm // halo
    vec = lambda i, t: (0, 0)
    return pl.pallas_call(
        functools.partial(_conv_prompt_body, tm=tm, halo=halo),
        grid=(bsz, l // tm),
        in_specs=[pl.BlockSpec((None, halo, B_CH), lambda i, t: (i, jnp.maximum(t * r - 1, 0), 0)),
                  pl.BlockSpec((None, tm, B_CH), lambda i, t: (i, t, 0)),
                  pl.BlockSpec(w.shape, vec), pl.BlockSpec((1, B_CH), vec), pl.BlockSpec((1, B_CH), vec),
                  pl.BlockSpec((1, B_CH), vec)],
        out_specs=pl.BlockSpec((None, tm, B_CH), lambda i, t: (i, t, 0)),
        out_shape=jax.ShapeDtypeStruct((bsz, l, B_CH), F32),
        scratch_shapes=[pltpu.VMEM((tm + halo, B_CH), F32), pltpu.VMEM((tm + halo, LANES), F32)],
        compiler_params=_params(("parallel", "arbitrary")),
        name="conv_prompt",
    )(u, u, w, b[None], g[None], bb[None])


def _conv_sample_body(up_ref, w_ref, b_ref, g_ref, bb_ref, o_ref, *, bt, s_len):
    kw = w_ref.shape[0]
    acc = jnp.zeros((bt, s_len, B_CH), F32) + b_ref[...]
    for j in range(kw):
        acc = acc + w_ref[j:j + 1, :] * up_ref[:, pl.ds(j, s_len), :]
    o_ref[...] = _silu(_ln(acc, g_ref[...], bb_ref[...]))


def _conv_sample(up, w, b, g, bb, bt=32):
    bsz, rows, _ = up.shape
    bt = min(bt, bsz)
    s_len = rows - (w.shape[0] - 1)
    vec = lambda i: (0, 0)
    return pl.pallas_call(
        functools.partial(_conv_sample_body, bt=bt, s_len=s_len),
        grid=(bsz // bt,),
        in_specs=[pl.BlockSpec((bt, rows, B_CH), lambda i: (i, 0, 0)),
                  pl.BlockSpec(w.shape, vec), pl.BlockSpec((1, B_CH), vec), pl.BlockSpec((1, B_CH), vec),
                  pl.BlockSpec((1, B_CH), vec)],
        out_specs=pl.BlockSpec((bt, s_len, B_CH), lambda i: (i, 0, 0)),
        out_shape=jax.ShapeDtypeStruct((bsz, s_len, B_CH), F32),
        compiler_params=_params(("parallel",)),
        name="conv_sample",
    )(up, w, b[None], g[None], bb[None])


def _hgrn_tile(hq, df, v, lb, ch):
    n = LANES // ch
    loglb, log1mlb, onemlb = lb[0:1, :], lb[1:2, :], lb[2:3, :]
    e = jnp.exp(-jnp.abs(df))
    ls = jnp.minimum(df, 0.0) - jnp.log1p(e)
    b_ = log1mlb + ls
    mx = jnp.maximum(loglb, b_)
    logf = mx + jnp.log(jnp.exp(loglb - mx) + jnp.exp(b_ - mx))
    hk = onemlb * (jnp.where(df >= 0.0, e, 1.0) / (1.0 + e))
    row = lax.broadcasted_iota(jnp.int32, (LANES, LANES), 0)
    colv = lax.broadcasted_iota(jnp.int32, (LANES, LANES), 1)
    r = row % ch
    g = logf
    sh = 1
    while sh < ch:
        g = g + jnp.where(r >= sh, pltpu.roll(g, sh, 0), 0.0)
        sh *= 2
    qa, kb, kd, qe, dec = [], [], [], [], []
    for c in range(n):
        sl = slice(ch * c, ch * (c + 1))
        gc = g[sl]
        gm = gc[ch // 2 - 1:ch // 2]
        gl = gc[ch - 1:ch]
        qa.append(hq[sl] * jnp.exp(gc - gm))
        kb.append(hk[sl] * jnp.exp(gm - gc))
        kd.append(hk[sl] * jnp.exp(gl - gc))
        qe.append((hq[sl] * jnp.exp(gc)).astype(BF16))
        dec.append(jnp.exp(gl))
    att = _nt(jnp.concatenate(qa, 0).astype(BF16), jnp.concatenate(kb, 0).astype(BF16))
    att = jnp.where((row // ch == colv // ch) & (colv <= row), att, 0.0)
    vb = v.astype(BF16)
    o_intra = _nn(att.astype(BF16), vb)
    vt = v.T
    kdb = jnp.concatenate(kd, 0).astype(BF16)
    ut = [_nn(jnp.where(colv // ch == c, vt, 0.0).astype(BF16), kdb) for c in range(n)]
    return o_intra, ut, qe, dec


def _hgrn_finish(o, sg, gn):
    return o * lax.rsqrt(jnp.mean(o * o, axis=-1, keepdims=True) + LN_EPS) * gn * sg


def _hgrn_prompt_body(hq_ref, df_ref, di_ref, sg_ref, lb_ref, gn_ref, o_ref, s_ref, *, ngroups, ch, nh):
    def group(gi, sts):
        off = pl.multiple_of(gi * LANES, LANES)
        rows = pl.ds(off, LANES)
        new = []
        for hh in range(nh):
            ln = slice(LANES * hh, LANES * (hh + 1))
            st = sts[hh]
            o_intra, ut, qe, dec = _hgrn_tile(hq_ref[rows, ln], df_ref[rows, ln], di_ref[rows, ln], lb_ref[:, ln], ch)
            o_inter = []
            for c in range(LANES // ch):
                o_inter.append(_nt(qe[c], st.astype(BF16)))
                st = dec[c] * st + ut[c]
            o = o_intra + jnp.concatenate(o_inter, 0)
            o_ref[rows, ln] = _hgrn_finish(o, sg_ref[rows, ln], gn_ref[...])
            new.append(st)
        return tuple(new)

    zero = jnp.zeros((LANES, LANES), F32)
    sts = lax.fori_loop(0, ngroups, group, (zero,) * nh)
    for hh in range(nh):
        s_ref[hh] = sts[hh].T


def _hgrn_prompt(d4, lbt, gn, ch=16, nh=2):
    b, l, _ = d4.shape
    hp = D_HEADS // nh
    seg = lambda s: pl.BlockSpec((None, l, nh * LANES), lambda i, h: (i, 0, hp * s + h))
    return pl.pallas_call(
        functools.partial(_hgrn_prompt_body, ngroups=l // LANES, ch=ch, nh=nh),
        grid=(b, hp),
        in_specs=[seg(0), seg(1), seg(2), seg(3),
                  pl.BlockSpec((3, nh * LANES), lambda i, h: (0, h)), pl.BlockSpec((1, LANES), lambda i, h: (0, 0))],
        out_specs=[pl.BlockSpec((None, l, nh * LANES), lambda i, h: (i, 0, h)),
                   pl.BlockSpec((None, nh, D_DK, LANES), lambda i, h: (i, h, 0, 0))],
        out_shape=[jax.ShapeDtypeStruct((b, l, D_HEADS * LANES), F32),
                   jax.ShapeDtypeStruct((b, D_HEADS, D_DK, LANES), F32)],
        compiler_params=_params(("parallel", "arbitrary")),
        name="hgrn_prompt",
    )(d4, d4, d4, d4, lbt, gn[None])


def _hgrn_sample_body(hq_ref, df_ref, di_ref, sg_ref, lb_ref, gn_ref, s0_ref, o_ref, s_ref, *, ch):
    o_intra, ut, qe, dec = _hgrn_tile(hq_ref[...], df_ref[...], di_ref[...], lb_ref[...], ch)
    o_inter = []
    for c in range(LANES // ch):
        st = s0_ref[c].T
        o_inter.append(_nt(qe[c], st.astype(BF16)))
        s_ref[c] = (dec[c] * st + ut[c]).T
    o = o_intra + jnp.concatenate(o_inter, 0)
    o_ref[...] = _hgrn_finish(o, sg_ref[...], gn_ref[...])


def _hgrn_sample(d4, lbt, gn, s0):
    m = d4.shape[0]
    b = s0.shape[0]
    s_len = m // b
    assert LANES % s_len == 0 and s_len & (s_len - 1) == 0 and m % LANES == 0
    bt = LANES // s_len
    seg = lambda s: pl.BlockSpec((LANES, LANES), lambda i, h: (i, D_HEADS * s + h))
    st_spec = pl.BlockSpec((bt, None, D_DK, LANES), lambda i, h: (i, h, 0, 0))
    return pl.pallas_call(
        functools.partial(_hgrn_sample_body, ch=s_len),
        grid=(m // LANES, D_HEADS),
        in_specs=[seg(0), seg(1), seg(2), seg(3),
                  pl.BlockSpec((3, LANES), lambda i, h: (0, h)), pl.BlockSpec((1, LANES), lambda i, h: (0, 0)), st_spec],
        out_specs=[pl.BlockSpec((LANES, LANES), lambda i, h: (i, h)), st_spec],
        out_shape=[jax.ShapeDtypeStruct((m, D_HEADS * LANES), F32), jax.ShapeDtypeStruct(s0.shape, F32)],
        compiler_params=_params(("parallel", "arbitrary")),
        name="hgrn_sample",
    )(d4, d4, d4, d4, lbt, gn[None], s0)


def _outproj_body(a_ref, c_ref, x_ref, w_ref, g_ref, b_ref, o_ref):
    wa = a_ref.shape[1]
    y = _nn(a_ref[...].astype(BF16), w_ref[0:wa, :]) + _nn(c_ref[...].astype(BF16), w_ref[wa:, :])
    o_ref[...] = _ln(ALPHA * x_ref[...] + y, g_ref[...], b_ref[...])


def _outproj_ln(a, c, x2, w_bf, g, b, tm):
    m = x2.shape[0]
    row = lambda i: (i, 0)
    vec = lambda i: (0, 0)
    return pl.pallas_call(
        _outproj_body,
        grid=(m // tm,),
        in_specs=[pl.BlockSpec((tm, a.shape[1]), row), pl.BlockSpec((tm, c.shape[1]), row),
                  pl.BlockSpec((tm, D_MODEL), row), pl.BlockSpec(w_bf.shape, vec),
                  pl.BlockSpec((1, D_MODEL), vec), pl.BlockSpec((1, D_MODEL), vec)],
        out_specs=pl.BlockSpec((tm, D_MODEL), row),
        out_shape=jax.ShapeDtypeStruct((m, D_MODEL), F32),
        compiler_params=_params(("parallel",)),
        name="outproj_ln",
    )(a, c, x2, w_bf, g[None], b[None])


def _ffn_body(x_ref, hist_ref, wg_ref, wu_ref, wd_ref, wc_ref, bc_ref, g_ref, b_ref,
              o_ref, st_ref, *carry, tm, seg, fc):
    x = x_ref[...]
    xb = x.astype(BF16)
    row = lax.broadcasted_iota(jnp.int32, (tm, fc), 0)
    r = row % seg
    if seg == tm:
        carry_ref, = carry

        @pl.when(pl.program_id(1) == 0)
        def _():
            carry_ref[...] = hist_ref[...]

    acc = jnp.zeros((tm, D_MODEL), F32)
    for c in range(D_FF // fc):
        cols = slice(fc * c, fc * (c + 1))
        gp = _nn(xb, wg_ref[:, cols])
        if seg == tm:
            h0 = carry_ref[0:1, cols]
            h1 = carry_ref[1:2, cols]
            p1 = jnp.where(row == 0, h1, pltpu.roll(gp, 1, 0))
            p2 = jnp.where(row == 0, h0, jnp.where(row == 1, h1, pltpu.roll(gp, 2, 0)))
            carry_ref[:, cols] = gp[tm - 2:tm]
            st_ref[:, cols] = gp[tm - 2:tm]
        else:
            hp = hist_ref[:, cols]
            p1 = jnp.where(r == 0, pltpu.roll(hp, (tm - (seg - 1)) % tm, 0), pltpu.roll(gp, 1, 0))
            p2 = jnp.where(r < 2, pltpu.roll(hp, (tm - (seg - 2)) % tm, 0), pltpu.roll(gp, 2, 0))
            st_ref[:, cols] = gp
        cv = wc_ref[0:1, cols] * p2 + wc_ref[1:2, cols] * p1 + wc_ref[2:3, cols] * gp + bc_ref[:, cols]
        hid = 0.5 * cv * (1.0 + lax.erf(cv * (2.0 ** -0.5))) * _nn(xb, wu_ref[:, cols])
        acc = acc + _nn(hid.astype(BF16), wd_ref[cols, :])
    o_ref[...] = _ln(ALPHA * x + acc, g_ref[...], b_ref[...])


def _ffn_prompt(x, hist, wg, wu, wd, wc, bc, g, b, tm=1024, fc=256):
    bsz, l, _ = x.shape
    tm = min(tm, l)
    assert l % tm == 0
    vec = lambda i, t: (0, 0)
    return pl.pallas_call(
        functools.partial(_ffn_body, tm=tm, seg=tm, fc=fc),
        grid=(bsz, l // tm),
        in_specs=[pl.BlockSpec((None, tm, D_MODEL), lambda i, t: (i, t, 0)),
                  pl.BlockSpec((None, 2, D_FF), lambda i, t: (i, 0, 0)),
                  pl.BlockSpec(wg.shape, vec), pl.BlockSpec(wu.shape, vec), pl.BlockSpec(wd.shape, vec),
                  pl.BlockSpec(wc.shape, vec), pl.BlockSpec((1, D_FF), vec),
                  pl.BlockSpec((1, D_MODEL), vec), pl.BlockSpec((1, D_MODEL), vec)],
        out_specs=[pl.BlockSpec((None, tm, D_MODEL), lambda i, t: (i, t, 0)),
                   pl.BlockSpec((None, 2, D_FF), lambda i, t: (i, 0, 0))],
        out_shape=[jax.ShapeDtypeStruct((bsz, l, D_MODEL), F32), jax.ShapeDtypeStruct((bsz, 2, D_FF), F32)],
        scratch_shapes=[pltpu.VMEM((FFN_CONV_WIDTH - 1, D_FF), F32)],
        compiler_params=_params(("parallel", "arbitrary")),
        name="ffn_prompt",
    )(x, hist, wg, wu, wd, wc, bc[None], g[None], b[None])


def _ffn_sample(x2, hist_rows, s_len, wg, wu, wd, wc, bc, g, b, tm=512, fc=256):
    m = x2.shape[0]
    tm = min(tm, m)
    vec = lambda i: (0, 0)
    row = lambda i: (i, 0)
    return pl.pallas_call(
        functools.partial(_ffn_body, tm=tm, seg=s_len, fc=fc),
        grid=(m // tm,),
        in_specs=[pl.BlockSpec((tm, D_MODEL), row), pl.BlockSpec((tm, D_FF), row),
                  pl.BlockSpec(wg.shape, vec), pl.BlockSpec(wu.shape, vec), pl.BlockSpec(wd.shape, vec),
                  pl.BlockSpec(wc.shape, vec), pl.BlockSpec((1, D_FF), vec),
                  pl.BlockSpec((1, D_MODEL), vec), pl.BlockSpec((1, D_MODEL), vec)],
        out_specs=[pl.BlockSpec((tm, D_MODEL), row), pl.BlockSpec((tm, D_FF), row)],
        out_shape=[jax.ShapeDtypeStruct((m, D_MODEL), F32), jax.ShapeDtypeStruct((m, D_FF), F32)],
        compiler_params=_params(("parallel",)),
        name="ffn_sample",
    )(x2, hist_rows, wg, wu, wd, wc, bc[None], g[None], b[None])


def _lb_tables(lb_param):
    lbs = jnp.cumsum(jax.nn.softmax(lb_param.astype(F32), axis=0), axis=0)
    lbs = lbs - lbs[:1]
    return jnp.stack([jnp.log(lbs), jnp.log1p(-lbs), 1.0 - lbs], axis=1)


def _trunk_prompt(x, prm, lbt):
    b, l, _ = x.shape
    m = b * l
    tm = 512
    tabs = _rope_tables(jnp.arange(l))
    kv_k, kv_v, conv_b, win_k, win_v, hgrn, ffn = [], [], [], [], [], [], []
    zero_hist = jnp.zeros((b, FFN_CONV_WIDTH - 1, D_FF), F32)
    for layer in range(DEPTH):
        i = layer // 2
        x2 = x.reshape(m, D_MODEL)
        if layer % 2 == 0:
            q, k, v, u = _inproj(x2, prm["w_in_even"][i], tabs, True, tm)
            k3, v3, u3 = k.reshape(b, l, KVW), v.reshape(b, l, KVW), u.reshape(b, l, B_CH)
            a = _moba_prompt(q.reshape(b, l, QW), k3, v3)
            cv = _conv_prompt(u3, prm["w_dw_b"][i], prm["b_dw_b"][i], prm["conv_ln_g"][i], prm["conv_ln_b"][i])
            kv_k.append(k3.reshape(b, l, N_KV, HEAD_DIM))
            kv_v.append(v3.reshape(b, l, N_KV, HEAD_DIM))
            conv_b.append(u3[:, l - (B_CONV_WIDTH - 1):])
            x1 = _outproj_ln(a.reshape(m, QW), cv.reshape(m, B_CH), x2, prm["w_out_even"][i],
                             prm["ln1_g"][layer], prm["ln1_b"][layer], tm)
        else:
            q, k, v, d4 = _inproj(x2, prm["w_in_odd"][i], tabs, False, tm)
            k3, v3 = k.reshape(b, l, KVW), v.reshape(b, l, KVW)
            a = _swa_prompt(q.reshape(b, l, QW), k3, v3, prm["sinks"][i])
            o, st = _hgrn_prompt(d4.reshape(b, l, 2048), lbt[i], prm["gnorm_g"][i])
            keep = min(WINDOW, l)
            win_k.append(k3[:, l - keep:].reshape(b, keep, N_KV, HEAD_DIM))
            win_v.append(v3[:, l - keep:].reshape(b, keep, N_KV, HEAD_DIM))
            hgrn.append(st)
            x1 = _outproj_ln(a.reshape(m, QW), o.reshape(m, 512), x2, prm["w_out_odd"][i],
                             prm["ln1_g"][layer], prm["ln1_b"][layer], tm)
        xo, fb = _ffn_prompt(x1.reshape(b, l, D_MODEL), zero_hist, prm["w_ffn_gate"][layer], prm["w_ffn_up"][layer],
                             prm["w_ffn_down"][layer], prm["w_ffn_conv"][layer], prm["b_ffn_conv"][layer],
                             prm["ln2_g"][layer], prm["ln2_b"][layer])
        ffn.append(fb)
        x = xo
    return (x, jnp.stack(kv_k), jnp.stack(kv_v), jnp.stack(conv_b), jnp.stack(win_k), jnp.stack(win_v),
            jnp.stack(hgrn), jnp.stack(ffn))


def _trunk_sample(x, prm, lbt, cache_k, cache_v, page_table, state_conv_b, state_win_k, state_win_v,
                  state_hgrn, state_ffn):
    b, s, _ = x.shape
    m = b * s
    tm = min(512, m)
    past = page_table.shape[1] * PAGE_SIZE
    assert state_win_k.shape[2] == WINDOW and past + s >= WINDOW
    tabs = _rope_tables(jnp.tile(past + jnp.arange(s), tm // s))
    pool_k_t = _kv_transposed(cache_k)
    pool_v_t = _kv_transposed(cache_v)
    win_k_t = _kv_transposed(state_win_k)
    win_v_t = _kv_transposed(state_win_v)
    kv_k, kv_v, conv_b, win_k, win_v, hgrn, ffn = [], [], [], [], [], [], []
    for layer in range(DEPTH):
        i = layer // 2
        x2 = x.reshape(m, D_MODEL)
        if layer % 2 == 0:
            q, k, v, u = _inproj(x2, prm["w_in_even"][i], tabs, True, tm)
            k3, v3 = k.reshape(b, s, KVW), v.reshape(b, s, KVW)
            a = _moba_sample(q.reshape(b, s, QW), k3, v3, pool_k_t, pool_v_t, page_table, i)
            up = jnp.concatenate([state_conv_b[i], u.reshape(b, s, B_CH)], 1)
            cv = _conv_sample(up, prm["w_dw_b"][i], prm["b_dw_b"][i], prm["conv_ln_g"][i], prm["conv_ln_b"][i])
            kv_k.append(k3.reshape(b, s, N_KV, HEAD_DIM))
            kv_v.append(v3.reshape(b, s, N_KV, HEAD_DIM))
            conv_b.append(up[:, up.shape[1] - (B_CONV_WIDTH - 1):])
            x1 = _outproj_ln(a.reshape(m, QW), cv.reshape(m, B_CH), x2, prm["w_out_even"][i],
                             prm["ln1_g"][layer], prm["ln1_b"][layer], tm)
        else:
            q, k, v, d4 = _inproj(x2, prm["w_in_odd"][i], tabs, False, tm)
            k3, v3 = k.reshape(b, s, KVW), v.reshape(b, s, KVW)
            a, nk_t, nv_t = _swa_sample(q.reshape(b, s, QW), k3, v3, win_k_t[i], win_v_t[i], prm["sinks"][i])
            o, st = _hgrn_sample(d4, lbt[i], prm["gnorm_g"][i], state_hgrn[i])
            win_k.append(_kv_untransposed(nk_t))
            win_v.append(_kv_untransposed(nv_t))
            hgrn.append(st)
            x1 = _outproj_ln(a.reshape(m, QW), o, x2, prm["w_out_odd"][i],
                             prm["ln1_g"][layer], prm["ln1_b"][layer], tm)
        hist_rows = jnp.pad(state_ffn[layer], ((0, 0), (s - (FFN_CONV_WIDTH - 1), 0), (0, 0))).reshape(m, D_FF)
        xo, gp = _ffn_sample(x1, hist_rows, s, prm["w_ffn_gate"][layer], prm["w_ffn_up"][layer],
                             prm["w_ffn_down"][layer], prm["w_ffn_conv"][layer], prm["b_ffn_conv"][layer],
                             prm["ln2_g"][layer], prm["ln2_b"][layer], tm=tm)
        ffn.append(gp.reshape(b, s, D_FF)[:, s - (FFN_CONV_WIDTH - 1):])
        x = xo.reshape(b, s, D_MODEL)
    return (x, jnp.stack(kv_k), jnp.stack(kv_v), jnp.stack(conv_b), jnp.stack(win_k), jnp.stack(win_v),
            jnp.stack(hgrn), jnp.stack(ffn))


def kernel(x_prompt, x_sample, cache_k, cache_v, state_conv_b, state_win_k, state_win_v, state_hgrn, state_ffn,
           page_table, w_in_even, w_dw_b, b_dw_b, conv_ln_g, conv_ln_b, w_out_even, w_in_odd, sinks, lb_param,
           gnorm_g, w_out_odd, ln1_g, ln1_b, ln2_g, ln2_b, w_ffn_gate, w_ffn_up, w_ffn_conv, b_ffn_conv, w_ffn_down):
    prm = dict(w_in_even=w_in_even.astype(BF16), w_dw_b=w_dw_b, b_dw_b=b_dw_b, conv_ln_g=conv_ln_g,
               conv_ln_b=conv_ln_b, w_out_even=w_out_even.astype(BF16), w_in_odd=w_in_odd.astype(BF16), sinks=sinks,
               gnorm_g=gnorm_g, w_out_odd=w_out_odd.astype(BF16), ln1_g=ln1_g, ln1_b=ln1_b, ln2_g=ln2_g, ln2_b=ln2_b,
               w_ffn_gate=w_ffn_gate.astype(BF16), w_ffn_up=w_ffn_up.astype(BF16), w_ffn_conv=w_ffn_conv,
               b_ffn_conv=b_ffn_conv, w_ffn_down=w_ffn_down.astype(BF16))
    lbt = _lb_tables(lb_param)
    yp, kkp, kvp, cbp, wkp, wvp, hgp, ffp = _trunk_prompt(x_prompt, prm, lbt)
    ys, kks, kvs, cbs, wks, wvs, hgs, ffs = _trunk_sample(
        x_sample, prm, lbt, cache_k, cache_v, page_table, state_conv_b, state_win_k, state_win_v, state_hgrn, state_ffn)
    return (yp, ys, kkp, kvp, kks, kvs, cbp, cbs, wkp, wvp, wks, wvs, hgp, hgs, ffp, ffs)
```

```python
import functools

import jax
import jax.numpy as jnp
from jax import lax
from jax.experimental import pallas as pl
from jax.experimental.pallas import tpu as pltpu

F32 = jnp.float32
BF16 = jnp.bfloat16

D_MODEL = 1024
DEPTH = 4
N_PAIR = DEPTH // 2
HEAD_DIM = 64
ROT_DIM = HEAD_DIM // 4
ROPE_THETA = 500000.0
N_HEADS = 8
N_KV = 2
GROUP = N_HEADS // N_KV
QW = N_HEADS * HEAD_DIM
KVW = N_KV * HEAD_DIM
MOBA_BLOCK = 256
MOBA_TOPK = 3
Q_CHUNK = 128
PAGE_SIZE = 128
B_CH = 512
B_CONV_WIDTH = 31
WINDOW = 128
D_HEADS = 4
D_DK = 128
D_FF = 2816
FFN_CONV_WIDTH = 3
LN_EPS = 1e-5
ALPHA = (2.0 * DEPTH) ** 0.25
SCALE = HEAD_DIM ** -0.5
EVEN_IN = QW + 2 * KVW + 2 * B_CH
ODD_IN = QW + 2 * KVW + 4 * 512

LANES = 128
NEG = -1e30
VMEM_LIMIT = 56 * 1024 * 1024

HIGHEST = lax.Precision.HIGHEST


def _nn(a, b, precision=None):
    return lax.dot_general(a, b, (((1,), (0,)), ((), ())), preferred_element_type=F32, precision=precision)


def _nt(a, b, precision=None):
    return lax.dot_general(a, b, (((1,), (1,)), ((), ())), preferred_element_type=F32, precision=precision)


def _ln(z, g, b):
    mu = jnp.mean(z, axis=-1, keepdims=True)
    d = z - mu
    var = jnp.mean(d * d, axis=-1, keepdims=True)
    return d * lax.rsqrt(var + LN_EPS) * g + b


def _silu(x):
    return x * jax.nn.sigmoid(x)


def _params(sem):
    return pltpu.CompilerParams(dimension_semantics=sem, vmem_limit_bytes=VMEM_LIMIT)


def _rope_tables(pos):
    half = ROT_DIM // 2
    inv = ROPE_THETA ** (-jnp.arange(half, dtype=F32) * 2.0 / ROT_DIM)
    ang = pos.astype(F32)[:, None] * inv[None, :]
    cos, sin = jnp.cos(ang), jnp.sin(ang)
    p = pos.shape[0]
    one = jnp.ones((p, HEAD_DIM - ROT_DIM), F32)
    zero = jnp.zeros((p, HEAD_DIM - ROT_DIM), F32)
    zh = jnp.zeros((p, half), F32)
    c = jnp.concatenate([cos, cos, one], 1)
    sa = jnp.concatenate([-sin, zh, zero], 1)
    sb = jnp.concatenate([zh, sin, zero], 1)
    return tuple(jnp.tile(t, (1, LANES // HEAD_DIM)) for t in (c, sa, sb))


def _rope(z, c, sa, sb):
    return z * c + pltpu.roll(z, LANES - ROT_DIM // 2, 1) * sa + pltpu.roll(z, ROT_DIM // 2, 1) * sb


def _inproj_even_body(x_ref, w_ref, c_ref, sa_ref, sb_ref, q_ref, k_ref, v_ref, u_ref):
    xb = x_ref[...].astype(BF16)
    c, sa, sb = c_ref[...], sa_ref[...], sb_ref[...]
    hq = _nn(xb, w_ref[:, 0:QW])
    for s in range(QW // LANES):
        q_ref[:, LANES * s:LANES * (s + 1)] = _rope(hq[:, LANES * s:LANES * (s + 1)], c, sa, sb)
    hkv = _nn(xb, w_ref[:, QW:QW + 2 * KVW])
    k_ref[...] = _rope(hkv[:, 0:KVW], c, sa, sb)
    v_ref[...] = hkv[:, KVW:2 * KVW]
    o = QW + 2 * KVW
    ga = _nn(xb, w_ref[:, o:o + B_CH])
    gb = _nn(xb, w_ref[:, o + B_CH:o + 2 * B_CH])
    u_ref[...] = ga * jax.nn.sigmoid(gb)


def _inproj_odd_body(x_ref, w_ref, c_ref, sa_ref, sb_ref, q_ref, k_ref, v_ref, d_ref):
    xb = x_ref[...].astype(BF16)
    c, sa, sb = c_ref[...], sa_ref[...], sb_ref[...]
    hq = _nn(xb, w_ref[:, 0:QW])
    for s in range(QW // LANES):
        q_ref[:, LANES * s:LANES * (s + 1)] = _rope(hq[:, LANES * s:LANES * (s + 1)], c, sa, sb)
    hkv = _nn(xb, w_ref[:, QW:QW + 2 * KVW])
    k_ref[...] = _rope(hkv[:, 0:KVW], c, sa, sb)
    v_ref[...] = hkv[:, KVW:2 * KVW]
    o = QW + 2 * KVW
    d_ref[:, 0:512] = _silu(_nn(xb, w_ref[:, o:o + 512]))
    d_ref[:, 512:1536] = _nn(xb, w_ref[:, o + 512:o + 1536])
    d_ref[:, 1536:2048] = _silu(_nn(xb, w_ref[:, o + 1536:o + 2048]))


def _inproj(x2, w_bf, tabs, even, tm):
    m = x2.shape[0]
    n_in = w_bf.shape[1]
    ntab = tabs[0].shape[0] // tm
    row = lambda i: (i, 0)
    tab = lambda i: (i % ntab, 0)
    wide = B_CH if even else 2048
    return pl.pallas_call(
        _inproj_even_body if even else _inproj_odd_body,
        grid=(m // tm,),
        in_specs=[pl.BlockSpec((tm, D_MODEL), row), pl.BlockSpec((D_MODEL, n_in), lambda i: (0, 0)),
                  pl.BlockSpec((tm, LANES), tab), pl.BlockSpec((tm, LANES), tab), pl.BlockSpec((tm, LANES), tab)],
        out_specs=[pl.BlockSpec((tm, QW), row), pl.BlockSpec((tm, KVW), row), pl.BlockSpec((tm, KVW), row),
                   pl.BlockSpec((tm, wide), row)],
        out_shape=[jax.ShapeDtypeStruct((m, QW), F32), jax.ShapeDtypeStruct((m, KVW), F32),
                   jax.ShapeDtypeStruct((m, KVW), F32), jax.ShapeDtypeStruct((m, wide), F32)],
        compiler_params=_params(("parallel",)),
        name="inproj_even" if even else "inproj_odd",
    )(x2, w_bf, *tabs)


def _queries_t(q):
    return jnp.concatenate([q[:, LANES * s:LANES * (s + 1)].T for s in range(QW // LANES)], 0)


def _stack_heads_t(qt, j):
    t = qt.shape[1]
    z = jnp.zeros((HEAD_DIM, t), F32)
    blocks = []
    for g in range(GROUP):
        h = GROUP * j + g
        hd = qt[HEAD_DIM * h:HEAD_DIM * (h + 1), :]
        blocks.append(jnp.concatenate([hd, z] if j == 0 else [z, hd], 0))
    return jnp.concatenate(blocks, 1)


def _unstack_heads(outs, t):
    lane = lax.broadcasted_iota(jnp.int32, (t, LANES), 1)
    slabs = []
    for p in range(N_HEADS // 2):
        halves = []
        for hh in range(2):
            h = 2 * p + hh
            j, g = h // GROUP, h % GROUP
            blk = outs[j][g * t:(g + 1) * t]
            if j != hh:
                blk = pltpu.roll(blk, HEAD_DIM, 1)
            halves.append(blk)
        slabs.append(jnp.where(lane < HEAD_DIM, halves[0], halves[1]))
    return jnp.concatenate(slabs, 1)


def _untranspose(o_t, t):
    return jnp.concatenate([o_t[:, t * g:t * (g + 1)].T for g in range(GROUP)], 0)


def _moba_prompt_body(q_ref, k_ref, v_ref, o_ref, km_ref, vt_ref, sel_ref, *, nb):
    c = pl.program_id(1)
    blk = c // (MOBA_BLOCK // Q_CHUNK)
    qs0 = c * Q_CHUNK
    cols = N_HEADS * Q_CHUNK
    nbp = km_ref.shape[0]

    @pl.when(c == 0)
    def _():
        km_ref[...] = jnp.concatenate(
            [jnp.mean(k_ref[n * MOBA_BLOCK:(n + 1) * MOBA_BLOCK, :], axis=0, keepdims=True) for n in range(nb)]
            + [jnp.zeros((nbp - nb, LANES), F32)] * (nbp > nb), 0)
        for n in range(nb):
            vt_ref[n] = v_ref[n * MOBA_BLOCK:(n + 1) * MOBA_BLOCK, :].T.astype(BF16)

    qt = _queries_t(q_ref[...] * SCALE)
    blk_row = lax.broadcasted_iota(jnp.int32, (nbp, cols), 0)
    kidx = lax.broadcasted_iota(jnp.int32, (MOBA_BLOCK, cols), 0)
    qpos = qs0 + lax.broadcasted_iota(jnp.int32, (MOBA_BLOCK, cols), 1) % Q_CHUNK
    qs_t = jnp.concatenate([_stack_heads_t(qt, j) for j in range(N_KV)], 1)
    qsb = qs_t.astype(BF16)
    gate = jnp.where(blk_row < blk, _nn(km_ref[...], qs_t, HIGHEST), -jnp.inf)
    sel_rows = []
    for n in range(nb):
        gn = gate[n:n + 1, :]
        beats = jnp.where((gate > gn) | ((gate == gn) & (blk_row < n)), 1.0, 0.0)
        cnt = jnp.sum(beats, axis=0, keepdims=True)
        sel_rows.append(jnp.where((cnt < MOBA_TOPK) & (n < blk), 1.0, 0.0))
    sel_ref[...] = jnp.concatenate(sel_rows + [jnp.zeros((nbp - nb, cols), F32)] * (nbp > nb), 0)

    def scores(n):
        off = pl.multiple_of(n * MOBA_BLOCK, MOBA_BLOCK)
        return _nn(k_ref[pl.ds(off, MOBA_BLOCK), :].astype(BF16), qsb), vt_ref[n]

    s, vt = scores(blk)
    s = jnp.where(kidx + blk * MOBA_BLOCK <= qpos, s, NEG)
    m = jnp.max(s, axis=0, keepdims=True)
    p = jnp.exp(s - m)
    l = jnp.sum(p, axis=0, keepdims=True)
    acc = _nn(vt, p.astype(BF16))

    def past(n, carry):
        m, l, acc = carry
        s, vt = scores(n)
        s = jnp.where(sel_ref[pl.ds(n, 1), :] > 0.5, s, NEG)
        m_new = jnp.maximum(m, jnp.max(s, axis=0, keepdims=True))
        a = jnp.exp(m - m_new)
        p = jnp.exp(s - m_new)
        return m_new, a * l + jnp.sum(p, axis=0, keepdims=True), a * acc + _nn(vt, p.astype(BF16))

    m, l, acc = lax.fori_loop(0, blk, past, (m, l, acc))
    o_t = acc / l
    half = GROUP * Q_CHUNK
    o_ref[...] = _unstack_heads([_untranspose(o_t[:, half * j:half * (j + 1)], Q_CHUNK) for j in range(N_KV)], Q_CHUNK)


def _moba_prompt(q, k, v):
    b, l, _ = q.shape
    nb = l // MOBA_BLOCK
    nbp = -(-nb // 8) * 8
    return pl.pallas_call(
        functools.partial(_moba_prompt_body, nb=nb),
        grid=(b, l // Q_CHUNK),
        in_specs=[pl.BlockSpec((None, Q_CHUNK, QW), lambda i, c: (i, c, 0)),
                  pl.BlockSpec((None, l, KVW), lambda i, c: (i, 0, 0)),
                  pl.BlockSpec((None, l, KVW), lambda i, c: (i, 0, 0))],
        out_specs=pl.BlockSpec((None, Q_CHUNK, QW), lambda i, c: (i, c, 0)),
        out_shape=jax.ShapeDtypeStruct((b, l, QW), F32),
        scratch_shapes=[pltpu.VMEM((nbp, LANES), F32), pltpu.VMEM((nb, KVW, MOBA_BLOCK), BF16),
                        pltpu.VMEM((nbp, N_HEADS * Q_CHUNK), F32)],
        compiler_params=_params(("parallel", "arbitrary")),
        name="moba_prompt",
    )(q, k, v)


def _stack_decode(q):
    s = q.shape[0]
    lane = lax.broadcasted_iota(jnp.int32, (s, LANES), 1)
    parts = []
    for h in range(N_HEADS):
        j = h // GROUP
        slab = q[:, LANES * (h // 2):LANES * (h // 2 + 1)]
        if h % 2 != j:
            slab = pltpu.roll(slab, HEAD_DIM, 1)
        keep = (lane < HEAD_DIM) if j == 0 else (lane >= HEAD_DIM)
        parts.append(jnp.where(keep, slab, 0.0))
    return jnp.concatenate(parts, 0)


def _unstack_decode(o, s):
    lane = lax.broadcasted_iota(jnp.int32, (s, LANES), 1)
    slabs = []
    for p in range(N_HEADS // 2):
        halves = []
        for hh in range(2):
            h = 2 * p + hh
            blk = o[h * s:(h + 1) * s]
            if h // GROUP != hh:
                blk = pltpu.roll(blk, HEAD_DIM, 1)
            halves.append(blk)
        slabs.append(jnp.where(lane < HEAD_DIM, halves[0], halves[1]))
    return jnp.concatenate(slabs, 1)


def _pad_rows(x, n):
    return jnp.concatenate([x, jnp.zeros((n - x.shape[0], x.shape[1]), x.dtype)], 0)


def _kv_transposed(x):
    lead = x.shape[:-3]
    n = len(lead)
    t = x.shape[-3]
    return jnp.transpose(x, tuple(range(n)) + (n + 1, n + 2, n)).reshape(lead + (KVW, t))


def _kv_untransposed(xt):
    lead = xt.shape[:-2]
    n = len(lead)
    t = xt.shape[-1]
    return jnp.transpose(xt.reshape(lead + (N_KV, HEAD_DIM, t)), tuple(range(n)) + (n + 2, n, n + 1))


def _moba_sample_body(pt_ref, q_ref, kn_ref, vn_ref, kpool_ref, vpool_ref, o_ref, kbuf_ref, vbuf_ref, kb_ref, s_ref,
                      sem_ref, *, nb, s_len, layer, nbatch):
    ppb = MOBA_BLOCK // PAGE_SIZE
    assert ppb == 2 and 2 * N_HEADS * s_len == LANES
    npages = nb * ppb
    rows = N_HEADS * s_len
    nbp = -(-nb // 8) * 8
    b = pl.program_id(0)
    slot = b % 2
    nslot = 1 - slot
    nxt = jnp.minimum(b + 1, nbatch - 1)

    def k_copy(bi, p, sl):
        return pltpu.make_async_copy(kpool_ref.at[layer, pt_ref[bi, p]], kbuf_ref.at[sl, p], sem_ref.at[sl, 0])

    def v_copy(bi, p, sl):
        return pltpu.make_async_copy(vpool_ref.at[layer, pt_ref[bi, p]], vbuf_ref.at[sl, p], sem_ref.at[sl, 1])

    @pl.when(b == 0)
    def _():
        for p in range(npages):
            k_copy(0, p, 0).start()
            v_copy(0, p, 0).start()

    for p in range(npages):
        k_copy(b, p, slot).wait()
        v_copy(b, p, slot).wait()

    qa = _stack_decode(q_ref[...] * SCALE)
    qt = jnp.concatenate([qa, qa], 0).T
    lane = lax.broadcasted_iota(jnp.int32, (LANES, LANES), 1)
    lo = lane < rows
    qt_lo = jnp.where(lo, qt, 0.0).astype(BF16)
    qt_hi = jnp.where(lo, 0.0, qt).astype(BF16)
    sums = []
    for p in range(npages):
        k_copy(nxt, p, nslot).start()
        v_copy(nxt, p, nslot).start()
        kpg = kbuf_ref[slot, p].T
        kb_ref[p] = kpg.astype(BF16)
        sums.append(jnp.sum(kpg, axis=0, keepdims=True))
    km = jnp.concatenate([sum(sums[ppb * n:ppb * (n + 1)]) * (1.0 / MOBA_BLOCK) for n in range(nb)]
                         + [jnp.zeros((nbp - nb, LANES), F32)] * (nbp > nb), 0)
    blk_row = lax.broadcasted_iota(jnp.int32, (nbp, LANES), 0)
    gate = jnp.where(blk_row < nb, _nn(km, qt, HIGHEST), -jnp.inf)
    sel = []
    for n in range(nb):
        gn = gate[n:n + 1, :]
        beats = jnp.where((gate > gn) | ((gate == gn) & (blk_row < n)), 1.0, 0.0)
        sel.append(jnp.sum(beats, axis=0, keepdims=True) < MOBA_TOPK)
    kidx = lax.broadcasted_iota(jnp.int32, (LANES, LANES), 0)
    tok = lane % s_len
    s_new = jnp.where((kidx <= tok) & (kidx < s_len) & lo, _nn(_pad_rows(kn_ref[...], LANES).astype(BF16), qt_lo), NEG)
    m = jnp.max(s_new, axis=0, keepdims=True)
    for n in range(nb):
        s = jnp.where(sel[n], _nn(kb_ref[ppb * n], qt_lo) + _nn(kb_ref[ppb * n + 1], qt_hi), NEG)
        s_ref[n] = s
        m = jnp.maximum(m, jnp.max(s, axis=0, keepdims=True))
    m = jnp.maximum(m, pltpu.roll(m, rows, 1))
    p_new = jnp.exp(s_new - m)
    l = jnp.sum(p_new, axis=0, keepdims=True)
    acc_lo = _nn(_pad_rows(vn_ref[...], LANES).T.astype(BF16), p_new.astype(BF16))
    acc_hi = jnp.zeros((KVW, LANES), F32)
    for n in range(nb):
        pp = jnp.exp(s_ref[n] - m)
        l = l + jnp.sum(pp, axis=0, keepdims=True)
        ppb16 = pp.astype(BF16)
        acc_lo = acc_lo + _nn(vbuf_ref[slot, ppb * n].astype(BF16), ppb16)
        acc_hi = acc_hi + _nn(vbuf_ref[slot, ppb * n + 1].astype(BF16), ppb16)
    acc = jnp.where(lo, acc_lo, acc_hi)
    o_t = (acc + pltpu.roll(acc, rows, 1)) / (l + pltpu.roll(l, rows, 1))
    o_ref[...] = _unstack_decode(o_t.T[0:rows], s_len)

    @pl.when(b == nbatch - 1)
    def _():
        for p in range(npages):
            k_copy(nxt, p, nslot).wait()
            v_copy(nxt, p, nslot).wait()


def _moba_sample(q, k_new, v_new, k_pool_t, v_pool_t, page_table, layer):
    b, s_len, _ = q.shape
    npages = page_table.shape[1]
    assert (npages * PAGE_SIZE) % MOBA_BLOCK == 0 and N_HEADS * s_len <= LANES
    nb = npages * PAGE_SIZE // MOBA_BLOCK
    assert 0 < nb <= LANES

    tok = lambda i, pt: (i, 0, 0)
    page_buf = pltpu.VMEM((2, npages, KVW, PAGE_SIZE), F32)
    grid_spec = pltpu.PrefetchScalarGridSpec(
        num_scalar_prefetch=1,
        grid=(b,),
        in_specs=[pl.BlockSpec((None, s_len, QW), tok), pl.BlockSpec((None, s_len, KVW), tok),
                  pl.BlockSpec((None, s_len, KVW), tok),
                  pl.BlockSpec(memory_space=pl.ANY), pl.BlockSpec(memory_space=pl.ANY)],
        out_specs=pl.BlockSpec((None, s_len, QW), tok),
        scratch_shapes=[page_buf, page_buf, pltpu.VMEM((npages, PAGE_SIZE, KVW), BF16),
                        pltpu.VMEM((nb, PAGE_SIZE, LANES), F32), pltpu.SemaphoreType.DMA((2, 2))],
    )
    return pl.pallas_call(
        functools.partial(_moba_sample_body, nb=nb, s_len=s_len, layer=layer, nbatch=b),
        grid_spec=grid_spec,
        out_shape=jax.ShapeDtypeStruct((b, s_len, QW), F32),
        compiler_params=_params(("arbitrary",)),
        name="moba_sample",
    )(page_table, q, k_new, v_new, k_pool_t, v_pool_t)


def _swa_prompt_body(q_ref, kp_ref, kc_ref, vp_ref, vc_ref, sink_ref, o_ref):
    t = pl.program_id(1)
    cols = N_HEADS * WINDOW
    qt = _queries_t(q_ref[...] * SCALE)
    kk = jnp.concatenate([kp_ref[...], kc_ref[...]], 0).astype(BF16)
    vv_t = jnp.concatenate([vp_ref[...].T, vc_ref[...].T], 1).astype(BF16)
    kidx = lax.broadcasted_iota(jnp.int32, (2 * WINDOW, cols), 0)
    tq = lax.broadcasted_iota(jnp.int32, (2 * WINDOW, cols), 1) % WINDOW
    lo = jnp.where(t == 0, WINDOW, 0)
    valid = (kidx >= tq) & (kidx <= tq + WINDOW) & (kidx >= lo)
    qs_t = jnp.concatenate([_stack_heads_t(qt, j) for j in range(N_KV)], 1)
    s = jnp.where(valid, _nn(kk, qs_t.astype(BF16)), NEG)
    sink = sink_ref[...]
    m = jnp.maximum(jnp.max(s, axis=0, keepdims=True), sink)
    p = jnp.exp(s - m)
    den = jnp.sum(p, axis=0, keepdims=True) + jnp.exp(sink - m)
    o_t = _nn(vv_t, p.astype(BF16)) / den
    half = GROUP * WINDOW
    o_ref[...] = _unstack_heads([_untranspose(o_t[:, half * j:half * (j + 1)], WINDOW) for j in range(N_KV)], WINDOW)


def _swa_prompt(q, k, v, sink):
    b, l, _ = q.shape
    sink_cols = jnp.repeat(sink.astype(F32), WINDOW)[None, :]
    cur = lambda i, t: (i, t, 0)
    prev = lambda i, t: (i, jnp.maximum(t - 1, 0), 0)
    return pl.pallas_call(
        _swa_prompt_body,
        grid=(b, l // WINDOW),
        in_specs=[pl.BlockSpec((None, WINDOW, QW), cur),
                  pl.BlockSpec((None, WINDOW, KVW), prev), pl.BlockSpec((None, WINDOW, KVW), cur),
                  pl.BlockSpec((None, WINDOW, KVW), prev), pl.BlockSpec((None, WINDOW, KVW), cur),
                  pl.BlockSpec((1, N_HEADS * WINDOW), lambda i, t: (0, 0))],
        out_specs=pl.BlockSpec((None, WINDOW, QW), cur),
        out_shape=jax.ShapeDtypeStruct((b, l, QW), F32),
        compiler_params=_params(("parallel", "arbitrary")),
        name="swa_prompt",
    )(q, k, k, v, v, sink_cols)


def _swa_sample_body(q_ref, kn_ref, vn_ref, wk_ref, wv_ref, sink_ref, o_ref, nk_ref, nv_ref, *, bt, s_len):
    rows = N_HEADS * s_len
    tok = lax.broadcasted_iota(jnp.int32, (rows, LANES), 0) % s_len
    lane = lax.broadcasted_iota(jnp.int32, (rows, LANES), 1)
    lane_w = lax.broadcasted_iota(jnp.int32, (KVW, WINDOW), 1)
    sink = sink_ref[...]

    def one(i, carry):
        qab = _stack_decode(q_ref[i] * SCALE).astype(BF16)
        kn = _pad_rows(kn_ref[i], LANES)
        vn = _pad_rows(vn_ref[i], LANES)
        wk_t = wk_ref[i]
        wv_t = wv_ref[i]
        s_win = jnp.where(lane >= tok, _nn(qab, wk_t.astype(BF16)), NEG)
        s_new = jnp.where((lane <= tok) & (lane < s_len), _nt(qab, kn.astype(BF16)), NEG)
        m = jnp.maximum(jnp.maximum(jnp.max(s_win, axis=1, keepdims=True), jnp.max(s_new, axis=1, keepdims=True)), sink)
        p_win = jnp.exp(s_win - m)
        p_new = jnp.exp(s_new - m)
        den = jnp.sum(p_win, axis=1, keepdims=True) + jnp.sum(p_new, axis=1, keepdims=True) + jnp.exp(sink - m)
        acc = _nt(p_win.astype(BF16), wv_t.astype(BF16)) + _nn(p_new.astype(BF16), vn.astype(BF16))
        o_ref[i] = _unstack_decode(acc / den, s_len)
        keep = lane_w < WINDOW - s_len
        nk_ref[i] = jnp.where(keep, pltpu.roll(wk_t, WINDOW - s_len, 1), pltpu.roll(kn.T, WINDOW - s_len, 1))
        nv_ref[i] = jnp.where(keep, pltpu.roll(wv_t, WINDOW - s_len, 1), pltpu.roll(vn.T, WINDOW - s_len, 1))
        return carry

    lax.fori_loop(0, bt, one, 0)


def _swa_sample(q, k_new, v_new, win_k_t, win_v_t, sink, bt=16):
    b, s_len, _ = q.shape
    bt = min(bt, b)
    assert win_k_t.shape[2] == WINDOW == LANES and b % bt == 0 and s_len <= WINDOW
    sink_rows = jnp.repeat(sink.astype(F32), s_len)[:, None]
    blk = lambda i: (i, 0, 0)
    win = pl.BlockSpec((bt, KVW, WINDOW), blk)
    return pl.pallas_call(
        functools.partial(_swa_sample_body, bt=bt, s_len=s_len),
        grid=(b // bt,),
        in_specs=[pl.BlockSpec((bt, s_len, QW), blk), pl.BlockSpec((bt, s_len, KVW), blk),
                  pl.BlockSpec((bt, s_len, KVW), blk), win, win,
                  pl.BlockSpec((N_HEADS * s_len, 1), lambda i: (0, 0))],
        out_specs=[pl.BlockSpec((bt, s_len, QW), blk), win, win],
        out_shape=[jax.ShapeDtypeStruct((b, s_len, QW), F32), jax.ShapeDtypeStruct((b, KVW, WINDOW), F32),
                   jax.ShapeDtypeStruct((b, KVW, WINDOW), F32)],
        compiler_params=_params(("parallel",)),
        name="swa_sample",
    )(q, k_new, v_new, win_k_t, win_v_t, sink_rows)


def _conv_prompt_body(prev_ref, cur_ref, w_ref, b_ref, g_ref, bb_ref, o_ref, ext_ref, sh_ref, *, tm, halo):
    t = pl.program_id(1)
    ext_ref[0:halo, :] = jnp.where(t == 0, 0.0, prev_ref[...])
    ext_ref[halo:halo + tm, :] = cur_ref[...]
    kw = w_ref.shape[0]
    base = halo - (kw - 1)
    sub = 8
    for lc in range(B_CH // LANES):
        ln = slice(LANES * lc, LANES * (lc + 1))
        acc = jnp.zeros((tm, LANES), F32) + b_ref[:, ln]
        for r in range(sub):
            offs = [o for o in range(base, base + kw) if o % sub == r]
            if not offs:
                continue
            n = tm + max(offs) - r
            if r:
                sh_ref[0:n, :] = ext_ref[pl.ds(r, n), ln]
            src = sh_ref if r else ext_ref.at[:, ln]
            for o in offs:
                acc = acc + w_ref[o - base:o - base + 1, ln] * src[pl.ds(o - r, tm), :]
        o_ref[:, ln] = acc
    o_ref[...] = _silu(_ln(o_ref[...], g_ref[...], bb_ref[...]))


def _conv_prompt(u, w, b, g, bb, tm=256, halo=32):
    bsz, l, _ = u.shape
    r = tm // halo
    vec = lambda i, t: (0, 0)
    return pl.pallas_call(
        functools.partial(_conv_prompt_body, tm=tm, halo=halo),
        grid=(bsz, l // tm),
        in_specs=[pl.BlockSpec((None, halo, B_CH), lambda i, t: (i, jnp.maximum(t * r - 1, 0), 0)),
                  pl.BlockSpec((None, tm, B_CH), lambda i, t: (i, t, 0)),
                  pl.BlockSpec(w.shape, vec), pl.BlockSpec((1, B_CH), vec), pl.BlockSpec((1, B_CH), vec),
                  pl.BlockSpec((1, B_CH), vec)],
        out_specs=pl.BlockSpec((None, tm, B_CH), lambda i, t: (i, t, 0)),
        out_shape=jax.ShapeDtypeStruct((bsz, l, B_CH), F32),
        scratch_shapes=[pltpu.VMEM((tm + halo, B_CH), F32), pltpu.VMEM((tm + halo, LANES), F32)],
        compiler_params=_params(("parallel", "arbitrary")),
        name="conv_prompt",
    )(u, u, w, b[None], g[None], bb[None])


def _conv_sample_body(up_ref, w_ref, b_ref, g_ref, bb_ref, o_ref, *, bt, s_len):
    kw = w_ref.shape[0]
    acc = jnp.zeros((bt, s_len, B_CH), F32) + b_ref[...]
    for j in range(kw):
        acc = acc + w_ref[j:j + 1, :] * up_ref[:, pl.ds(j, s_len), :]
    o_ref[...] = _silu(_ln(acc, g_ref[...], bb_ref[...]))


def _conv_sample(up, w, b, g, bb, bt=32):
    bsz, rows, _ = up.shape
    bt = min(bt, bsz)
    s_len = rows - (w.shape[0] - 1)
    vec = lambda i: (0, 0)
    return pl.pallas_call(
        functools.partial(_conv_sample_body, bt=bt, s_len=s_len),
        grid=(bsz // bt,),
        in_specs=[pl.BlockSpec((bt, rows, B_CH), lambda i: (i, 0, 0)),
                  pl.BlockSpec(w.shape, vec), pl.BlockSpec((1, B_CH), vec), pl.BlockSpec((1, B_CH), vec),
                  pl.BlockSpec((1, B_CH), vec)],
        out_specs=pl.BlockSpec((bt, s_len, B_CH), lambda i: (i, 0, 0)),
        out_shape=jax.ShapeDtypeStruct((bsz, s_len, B_CH), F32),
        compiler_params=_params(("parallel",)),
        name="conv_sample",
    )(up, w, b[None], g[None], bb[None])


def _hgrn_tile(hq, df, v, lb, ch):
    n = LANES // ch
    loglb, log1mlb, onemlb = lb[0:1, :], lb[1:2, :], lb[2:3, :]
    e = jnp.exp(-jnp.abs(df))
    ls = jnp.minimum(df, 0.0) - jnp.log1p(e)
    b_ = log1mlb + ls
    mx = jnp.maximum(loglb, b_)
    logf = mx + jnp.log(jnp.exp(loglb - mx) + jnp.exp(b_ - mx))
    hk = onemlb * (jnp.where(df >= 0.0, e, 1.0) / (1.0 + e))
    row = lax.broadcasted_iota(jnp.int32, (LANES, LANES), 0)
    colv = lax.broadcasted_iota(jnp.int32, (LANES, LANES), 1)
    r = row % ch
    g = logf
    sh = 1
    while sh < ch:
        g = g + jnp.where(r >= sh, pltpu.roll(g, sh, 0), 0.0)
        sh *= 2
    qa, kb, kd, qe, dec = [], [], [], [], []
    for c in range(n):
        sl = slice(ch * c, ch * (c + 1))
        gc = g[sl]
        gm = gc[ch // 2 - 1:ch // 2]
        gl = gc[ch - 1:ch]
        qa.append(hq[sl] * jnp.exp(gc - gm))
        kb.append(hk[sl] * jnp.exp(gm - gc))
        kd.append(hk[sl] * jnp.exp(gl - gc))
        qe.append((hq[sl] * jnp.exp(gc)).astype(BF16))
        dec.append(jnp.exp(gl))
    att = _nt(jnp.concatenate(qa, 0).astype(BF16), jnp.concatenate(kb, 0).astype(BF16))
    att = jnp.where((row // ch == colv // ch) & (colv <= row), att, 0.0)
    vb = v.astype(BF16)
    o_intra = _nn(att.astype(BF16), vb)
    vt = v.T
    kdb = jnp.concatenate(kd, 0).astype(BF16)
    ut = [_nn(jnp.where(colv // ch == c, vt, 0.0).astype(BF16), kdb) for c in range(n)]
    return o_intra, ut, qe, dec


def _hgrn_finish(o, sg, gn):
    return o * lax.rsqrt(jnp.mean(o * o, axis=-1, keepdims=True) + LN_EPS) * gn * sg


def _hgrn_prompt_body(hq_ref, df_ref, di_ref, sg_ref, lb_ref, gn_ref, o_ref, s_ref, *, ngroups, ch, nh):
    def group(gi, sts):
        off = pl.multiple_of(gi * LANES, LANES)
        rows = pl.ds(off, LANES)
        new = []
        for hh in range(nh):
            ln = slice(LANES * hh, LANES * (hh + 1))
            st = sts[hh]
            o_intra, ut, qe, dec = _hgrn_tile(hq_ref[rows, ln], df_ref[rows, ln], di_ref[rows, ln], lb_ref[:, ln], ch)
            o_inter = []
            for c in range(LANES // ch):
                o_inter.append(_nt(qe[c], st.astype(BF16)))
                st = dec[c] * st + ut[c]
            o = o_intra + jnp.concatenate(o_inter, 0)
            o_ref[rows, ln] = _hgrn_finish(o, sg_ref[rows, ln], gn_ref[...])
            new.append(st)
        return tuple(new)

    zero = jnp.zeros((LANES, LANES), F32)
    sts = lax.fori_loop(0, ngroups, group, (zero,) * nh)
    for hh in range(nh):
        s_ref[hh] = sts[hh].T


def _hgrn_prompt(d4, lbt, gn, ch=16, nh=2):
    b, l, _ = d4.shape
    hp = D_HEADS // nh
    seg = lambda s: pl.BlockSpec((None, l, nh * LANES), lambda i, h: (i, 0, hp * s + h))
    return pl.pallas_call(
        functools.partial(_hgrn_prompt_body, ngroups=l // LANES, ch=ch, nh=nh),
        grid=(b, hp),
        in_specs=[seg(0), seg(1), seg(2), seg(3),
                  pl.BlockSpec((3, nh * LANES), lambda i, h: (0, h)), pl.BlockSpec((1, LANES), lambda i, h: (0, 0))],
        out_specs=[pl.BlockSpec((None, l, nh * LANES), lambda i, h: (i, 0, h)),
                   pl.BlockSpec((None, nh, D_DK, LANES), lambda i, h: (i, h, 0, 0))],
        out_shape=[jax.ShapeDtypeStruct((b, l, D_HEADS * LANES), F32),
                   jax.ShapeDtypeStruct((b, D_HEADS, D_DK, LANES), F32)],
        compiler_params=_params(("parallel", "arbitrary")),
        name="hgrn_prompt",
    )(d4, d4, d4, d4, lbt, gn[None])


def _hgrn_sample_body(hq_ref, df_ref, di_ref, sg_ref, lb_ref, gn_ref, s0_ref, o_ref, s_ref, *, ch):
    o_intra, ut, qe, dec = _hgrn_tile(hq_ref[...], df_ref[...], di_ref[...], lb_ref[...], ch)
    o_inter = []
    for c in range(LANES // ch):
        st = s0_ref[c].T
        o_inter.append(_nt(qe[c], st.astype(BF16)))
        s_ref[c] = (dec[c] * st + ut[c]).T
    o = o_intra + jnp.concatenate(o_inter, 0)
    o_ref[...] = _hgrn_finish(o, sg_ref[...], gn_ref[...])


def _hgrn_sample(d4, lbt, gn, s0):
    m = d4.shape[0]
    b = s0.shape[0]
    s_len = m // b
    assert LANES % s_len == 0 and s_len & (s_len - 1) == 0 and m % LANES == 0
    bt = LANES // s_len
    seg = lambda s: pl.BlockSpec((LANES, LANES), lambda i, h: (i, D_HEADS * s + h))
    st_spec = pl.BlockSpec((bt, None, D_DK, LANES), lambda i, h: (i, h, 0, 0))
    return pl.pallas_call(
        functools.partial(_hgrn_sample_body, ch=s_len),
        grid=(m // LANES, D_HEADS),
        in_specs=[seg(0), seg(1), seg(2), seg(3),
                  pl.BlockSpec((3, LANES), lambda i, h: (0, h)), pl.BlockSpec((1, LANES), lambda i, h: (0, 0)), st_spec],
        out_specs=[pl.BlockSpec((LANES, LANES), lambda i, h: (i, h)), st_spec],
        out_shape=[jax.ShapeDtypeStruct((m, D_HEADS * LANES), F32), jax.ShapeDtypeStruct(s0.shape, F32)],
        compiler_params=_params(("parallel", "arbitrary")),
        name="hgrn_sample",
    )(d4, d4, d4, d4, lbt, gn[None], s0)


def _outproj_body(a_ref, c_ref, x_ref, w_ref, g_ref, b_ref, o_ref):
    wa = a_ref.shape[1]
    y = _nn(a_ref[...].astype(BF16), w_ref[0:wa, :]) + _nn(c_ref[...].astype(BF16), w_ref[wa:, :])
    o_ref[...] = _ln(ALPHA * x_ref[...] + y, g_ref[...], b_ref[...])


def _outproj_ln(a, c, x2, w_bf, g, b, tm):
    m = x2.shape[0]
    row = lambda i: (i, 0)
    vec = lambda i: (0, 0)
    return pl.pallas_call(
        _outproj_body,
        grid=(m // tm,),
        in_specs=[pl.BlockSpec((tm, a.shape[1]), row), pl.BlockSpec((tm, c.shape[1]), row),
                  pl.BlockSpec((tm, D_MODEL), row), pl.BlockSpec(w_bf.shape, vec),
                  pl.BlockSpec((1, D_MODEL), vec), pl.BlockSpec((1, D_MODEL), vec)],
        out_specs=pl.BlockSpec((tm, D_MODEL), row),
        out_shape=jax.ShapeDtypeStruct((m, D_MODEL), F32),
        compiler_params=_params(("parallel",)),
        name="outproj_ln",
    )(a, c, x2, w_bf, g[None], b[None])


def _ffn_body(x_ref, hist_ref, wg_ref, wu_ref, wd_ref, wc_ref, bc_ref, g_ref, b_ref,
              o_ref, st_ref, *carry, tm, seg, fc):
    x = x_ref[...]
    xb = x.astype(BF16)
    row = lax.broadcasted_iota(jnp.int32, (tm, fc), 0)
    r = row % seg
    if seg == tm:
        carry_ref, = carry

        @pl.when(pl.program_id(1) == 0)
        def _():
            carry_ref[...] = hist_ref[...]

    acc = jnp.zeros((tm, D_MODEL), F32)
    for c in range(D_FF // fc):
        cols = slice(fc * c, fc * (c + 1))
        gp = _nn(xb, wg_ref[:, cols])
        if seg == tm:
            h0 = carry_ref[0:1, cols]
            h1 = carry_ref[1:2, cols]
            p1 = jnp.where(row == 0, h1, pltpu.roll(gp, 1, 0))
            p2 = jnp.where(row == 0, h0, jnp.where(row == 1, h1, pltpu.roll(gp, 2, 0)))
            carry_ref[:, cols] = gp[tm - 2:tm]
            st_ref[:, cols] = gp[tm - 2:tm]
        else:
            hp = hist_ref[:, cols]
            p1 = jnp.where(r == 0, pltpu.roll(hp, (tm - (seg - 1)) % tm, 0), pltpu.roll(gp, 1, 0))
            p2 = jnp.where(r < 2, pltpu.roll(hp, (tm - (seg - 2)) % tm, 0), pltpu.roll(gp, 2, 0))
            st_ref[:, cols] = gp
        cv = wc_ref[0:1, cols] * p2 + wc_ref[1:2, cols] * p1 + wc_ref[2:3, cols] * gp + bc_ref[:, cols]
        hid = 0.5 * cv * (1.0 + lax.erf(cv * (2.0 ** -0.5))) * _nn(xb, wu_ref[:, cols])
        acc = acc + _nn(hid.astype(BF16), wd_ref[cols, :])
    o_ref[...] = _ln(ALPHA * x + acc, g_ref[...], b_ref[...])


def _ffn_prompt(x, hist, wg, wu, wd, wc, bc, g, b, tm=1024, fc=256):
    bsz, l, _ = x.shape
    tm = min(tm, l)
    assert l % tm == 0
    vec = lambda i, t: (0, 0)
    return pl.pallas_call(
        functools.partial(_ffn_body, tm=tm, seg=tm, fc=fc),
        grid=(bsz, l // tm),
        in_specs=[pl.BlockSpec((None, tm, D_MODEL), lambda i, t: (i, t, 0)),
                  pl.BlockSpec((None, 2, D_FF), lambda i, t: (i, 0, 0)),
                  pl.BlockSpec(wg.shape, vec), pl.BlockSpec(wu.shape, vec), pl.BlockSpec(wd.shape, vec),
                  pl.BlockSpec(wc.shape, vec), pl.BlockSpec((1, D_FF), vec),
                  pl.BlockSpec((1, D_MODEL), vec), pl.BlockSpec((1, D_MODEL), vec)],
        out_specs=[pl.BlockSpec((None, tm, D_MODEL), lambda i, t: (i, t, 0)),
                   pl.BlockSpec((None, 2, D_FF), lambda i, t: (i, 0, 0))],
        out_shape=[jax.ShapeDtypeStruct((bsz, l, D_MODEL), F32), jax.ShapeDtypeStruct((bsz, 2, D_FF), F32)],
        scratch_shapes=[pltpu.VMEM((FFN_CONV_WIDTH - 1, D_FF), F32)],
        compiler_params=_params(("parallel", "arbitrary")),
        name="ffn_prompt",
    )(x, hist, wg, wu, wd, wc, bc[None], g[None], b[None])


def _ffn_sample(x2, hist_rows, s_len, wg, wu, wd, wc, bc, g, b, tm=512, fc=256):
    m = x2.shape[0]
    tm = min(tm, m)
    vec = lambda i: (0, 0)
    row = lambda i: (i, 0)
    return pl.pallas_call(
        functools.partial(_ffn_body, tm=tm, seg=s_len, fc=fc),
        grid=(m // tm,),
        in_specs=[pl.BlockSpec((tm, D_MODEL), row), pl.BlockSpec((tm, D_FF), row),
                  pl.BlockSpec(wg.shape, vec), pl.BlockSpec(wu.shape, vec), pl.BlockSpec(wd.shape, vec),
                  pl.BlockSpec(wc.shape, vec), pl.BlockSpec((1, D_FF), vec),
                  pl.BlockSpec((1, D_MODEL), vec), pl.BlockSpec((1, D_MODEL), vec)],
        out_specs=[pl.BlockSpec((tm, D_MODEL), row), pl.BlockSpec((tm, D_FF), row)],
        out_shape=[jax.ShapeDtypeStruct((m, D_MODEL), F32), jax.ShapeDtypeStruct((m, D_FF), F32)],
        compiler_params=_params(("parallel",)),
        name="ffn_sample",
    )(x2, hist_rows, wg, wu, wd, wc, bc[None], g[None], b[None])


def _lb_tables(lb_param):
    lbs = jnp.cumsum(jax.nn.softmax(lb_param.astype(F32), axis=0), axis=0)
    lbs = lbs - lbs[:1]
    return jnp.stack([jnp.log(lbs), jnp.log1p(-lbs), 1.0 - lbs], axis=1)


def _trunk_prompt(x, prm, lbt):
    b, l, _ = x.shape
    m = b * l
    tm = 512
    tabs = _rope_tables(jnp.arange(l))
    kv_k, kv_v, conv_b, win_k, win_v, hgrn, ffn = [], [], [], [], [], [], []
    zero_hist = jnp.zeros((b, FFN_CONV_WIDTH - 1, D_FF), F32)
    for layer in range(DEPTH):
        i = layer // 2
        x2 = x.reshape(m, D_MODEL)
        if layer % 2 == 0:
            q, k, v, u = _inproj(x2, prm["w_in_even"][i], tabs, True, tm)
            k3, v3, u3 = k.reshape(b, l, KVW), v.reshape(b, l, KVW), u.reshape(b, l, B_CH)
            a = _moba_prompt(q.reshape(b, l, QW), k3, v3)
            cv = _conv_prompt(u3, prm["w_dw_b"][i], prm["b_dw_b"][i], prm["conv_ln_g"][i], prm["conv_ln_b"][i])
            kv_k.append(k3.reshape(b, l, N_KV, HEAD_DIM))
            kv_v.append(v3.reshape(b, l, N_KV, HEAD_DIM))
            conv_b.append(u3[:, l - (B_CONV_WIDTH - 1):])
            x1 = _outproj_ln(a.reshape(m, QW), cv.reshape(m, B_CH), x2, prm["w_out_even"][i],
                             prm["ln1_g"][layer], prm["ln1_b"][layer], tm)
        else:
            q, k, v, d4 = _inproj(x2, prm["w_in_odd"][i], tabs, False, tm)
            k3, v3 = k.reshape(b, l, KVW), v.reshape(b, l, KVW)
            a = _swa_prompt(q.reshape(b, l, QW), k3, v3, prm["sinks"][i])
            o, st = _hgrn_prompt(d4.reshape(b, l, 2048), lbt[i], prm["gnorm_g"][i])
            keep = min(WINDOW, l)
            win_k.append(k3[:, l - keep:].reshape(b, keep, N_KV, HEAD_DIM))
            win_v.append(v3[:, l - keep:].reshape(b, keep, N_KV, HEAD_DIM))
            hgrn.append(st)
            x1 = _outproj_ln(a.reshape(m, QW), o.reshape(m, 512), x2, prm["w_out_odd"][i],
                             prm["ln1_g"][layer], prm["ln1_b"][layer], tm)
        xo, fb = _ffn_prompt(x1.reshape(b, l, D_MODEL), zero_hist, prm["w_ffn_gate"][layer], prm["w_ffn_up"][layer],
                             prm["w_ffn_down"][layer], prm["w_ffn_conv"][layer], prm["b_ffn_conv"][layer],
                             prm["ln2_g"][layer], prm["ln2_b"][layer])
        ffn.append(fb)
        x = xo
    return (x, jnp.stack(kv_k), jnp.stack(kv_v), jnp.stack(conv_b), jnp.stack(win_k), jnp.stack(win_v),
            jnp.stack(hgrn), jnp.stack(ffn))


def _trunk_sample(x, prm, lbt, cache_k, cache_v, page_table, state_conv_b, state_win_k, state_win_v,
                  state_hgrn, state_ffn):
    b, s, _ = x.shape
    m = b * s
    tm = min(512, m)
    past = page_table.shape[1] * PAGE_SIZE
    assert state_win_k.shape[2] == WINDOW and past + s >= WINDOW
    tabs = _rope_tables(jnp.tile(past + jnp.arange(s), tm // s))
    pool_k_t = _kv_transposed(cache_k)
    pool_v_t = _kv_transposed(cache_v)
    win_k_t = _kv_transposed(state_win_k)
    win_v_t = _kv_transposed(state_win_v)
    kv_k, kv_v, conv_b, win_k, win_v, hgrn, ffn = [], [], [], [], [], [], []
    for layer in range(DEPTH):
        i = layer // 2
        x2 = x.reshape(m, D_MODEL)
        if layer % 2 == 0:
            q, k, v, u = _inproj(x2, prm["w_in_even"][i], tabs, True, tm)
            k3, v3 = k.reshape(b, s, KVW), v.reshape(b, s, KVW)
            a = _moba_sample(q.reshape(b, s, QW), k3, v3, pool_k_t, pool_v_t, page_table, i)
            up = jnp.concatenate([state_conv_b[i], u.reshape(b, s, B_CH)], 1)
            cv = _conv_sample(up, prm["w_dw_b"][i], prm["b_dw_b"][i], prm["conv_ln_g"][i], prm["conv_ln_b"][i])
            kv_k.append(k3.reshape(b, s, N_KV, HEAD_DIM))
            kv_v.append(v3.reshape(b, s, N_KV, HEAD_DIM))
            conv_b.append(up[:, up.shape[1] - (B_CONV_WIDTH - 1):])
            x1 = _outproj_ln(a.reshape(m, QW), cv.reshape(m, B_CH), x2, prm["w_out_even"][i],
                             prm["ln1_g"][layer], prm["ln1_b"][layer], tm)
        else:
            q, k, v, d4 = _inproj(x2, prm["w_in_odd"][i], tabs, False, tm)
            k3, v3 = k.reshape(b, s, KVW), v.reshape(b, s, KVW)
            a, nk_t, nv_t = _swa_sample(q.reshape(b, s, QW), k3, v3, win_k_t[i], win_v_t[i], prm["sinks"][i])
            o, st = _hgrn_sample(d4, lbt[i], prm["gnorm_g"][i], state_hgrn[i])
            win_k.append(_kv_untransposed(nk_t))
            win_v.append(_kv_untransposed(nv_t))
            hgrn.append(st)
            x1 = _outproj_ln(a.reshape(m, QW), o, x2, prm["w_out_odd"][i],
                             prm["ln1_g"][layer], prm["ln1_b"][layer], tm)
        hist_rows = jnp.pad(state_ffn[layer], ((0, 0), (s - (FFN_CONV_WIDTH - 1), 0), (0, 0))).reshape(m, D_FF)
        xo, gp = _ffn_sample(x1, hist_rows, s, prm["w_ffn_gate"][layer], prm["w_ffn_up"][layer],
                             prm["w_ffn_down"][layer], prm["w_ffn_conv"][layer], prm["b_ffn_conv"][layer],
                             prm["ln2_g"][layer], prm["ln2_b"][layer], tm=tm)
        ffn.append(gp.reshape(b, s, D_FF)[:, s - (FFN_CONV_WIDTH - 1):])
        x = xo.reshape(b, s, D_MODEL)
    return (x, jnp.stack(kv_k), jnp.stack(kv_v), jnp.stack(conv_b), jnp.stack(win_k), jnp.stack(win_v),
            jnp.stack(hgrn), jnp.stack(ffn))


def kernel(x_prompt, x_sample, cache_k, cache_v, state_conv_b, state_win_k, state_win_v, state_hgrn, state_ffn,
           page_table, w_in_even, w_dw_b, b_dw_b, conv_ln_g, conv_ln_b, w_out_even, w_in_odd, sinks, lb_param,
           gnorm_g, w_out_odd, ln1_g, ln1_b, ln2_g, ln2_b, w_ffn_gate, w_ffn_up, w_ffn_conv, b_ffn_conv, w_ffn_down):
    prm = dict(w_in_even=w_in_even.astype(BF16), w_dw_b=w_dw_b, b_dw_b=b_dw_b, conv_ln_g=conv_ln_g,
               conv_ln_b=conv_ln_b, w_out_even=w_out_even.astype(BF16), w_in_odd=w_in_odd.astype(BF16), sinks=sinks,
               gnorm_g=gnorm_g, w_out_odd=w_out_odd.astype(BF16), ln1_g=ln1_g, ln1_b=ln1_b, ln2_g=ln2_g, ln2_b=ln2_b,
               w_ffn_gate=w_ffn_gate.astype(BF16), w_ffn_up=w_ffn_up.astype(BF16), w_ffn_conv=w_ffn_conv,
               b_ffn_conv=b_ffn_conv, w_ffn_down=w_ffn_down.astype(BF16))
    lbt = _lb_tables(lb_param)
    yp, kkp, kvp, cbp, wkp, wvp, hgp, ffp = _trunk_prompt(x_prompt, prm, lbt)
    ys, kks, kvs, cbs, wks, wvs, hgs, ffs = _trunk_sample(
        x_sample, prm, lbt, cache_k, cache_v, page_table, state_conv_b, state_win_k, state_win_v, state_hgrn, state_ffn)
    return (yp, ys, kkp, kvp, kks, kvs, cbp, cbs, wkp, wvp, wks, wvs, hgp, hgs, ffp, ffs)
```

```python
import functools

import jax
import jax.numpy as jnp
from jax import lax
from jax.experimental import pallas as pl
from jax.experimental.pallas import tpu as pltpu

F32 = jnp.float32
BF16 = jnp.bfloat16

D_MODEL = 1024
DEPTH = 4
N_PAIR = DEPTH // 2
HEAD_DIM = 64
ROT_DIM = HEAD_DIM // 4
ROPE_THETA = 500000.0
N_HEADS = 8
N_KV = 2
GROUP = N_HEADS // N_KV
QW = N_HEADS * HEAD_DIM
KVW = N_KV * HEAD_DIM
MOBA_BLOCK = 256
MOBA_TOPK = 3
Q_CHUNK = 128
PAGE_SIZE = 128
B_CH = 512
B_CONV_WIDTH = 31
WINDOW = 128
D_HEADS = 4
D_DK = 128
D_FF = 2816
FFN_CONV_WIDTH = 3
LN_EPS = 1e-5
ALPHA = (2.0 * DEPTH) ** 0.25
SCALE = HEAD_DIM ** -0.5
EVEN_IN = QW + 2 * KVW + 2 * B_CH
ODD_IN = QW + 2 * KVW + 4 * 512

LANES = 128
NEG = -1e30
VMEM_LIMIT = 56 * 1024 * 1024

HIGHEST = lax.Precision.HIGHEST


def _nn(a, b, precision=None):
    return lax.dot_general(a, b, (((1,), (0,)), ((), ())), preferred_element_type=F32, precision=precision)


def _nt(a, b, precision=None):
    return lax.dot_general(a, b, (((1,), (1,)), ((), ())), preferred_element_type=F32, precision=precision)


def _ln(z, g, b):
    mu = jnp.mean(z, axis=-1, keepdims=True)
    d = z - mu
    var = jnp.mean(d * d, axis=-1, keepdims=True)
    return d * lax.rsqrt(var + LN_EPS) * g + b


def _silu(x):
    return x * jax.nn.sigmoid(x)


def _params(sem):
    return pltpu.CompilerParams(dimension_semantics=sem, vmem_limit_bytes=VMEM_LIMIT)


def _layer_spec(w, layer):
    zeros = (0,) * (w.ndim - 1)
    return pl.BlockSpec((None,) + w.shape[1:], lambda *_: (layer,) + zeros)


def _rope_tables(pos):
    half = ROT_DIM // 2
    inv = ROPE_THETA ** (-jnp.arange(half, dtype=F32) * 2.0 / ROT_DIM)
    ang = pos.astype(F32)[:, None] * inv[None, :]
    cos, sin = jnp.cos(ang), jnp.sin(ang)
    p = pos.shape[0]
    one = jnp.ones((p, HEAD_DIM - ROT_DIM), F32)
    zero = jnp.zeros((p, HEAD_DIM - ROT_DIM), F32)
    zh = jnp.zeros((p, half), F32)
    c = jnp.concatenate([cos, cos, one], 1)
    sa = jnp.concatenate([-sin, zh, zero], 1)
    sb = jnp.concatenate([zh, sin, zero], 1)
    return tuple(jnp.tile(t, (1, LANES // HEAD_DIM)) for t in (c, sa, sb))


def _rope(z, c, sa, sb):
    return z * c + pltpu.roll(z, LANES - ROT_DIM // 2, 1) * sa + pltpu.roll(z, ROT_DIM // 2, 1) * sb


def _inproj_even_body(x_ref, w_ref, c_ref, sa_ref, sb_ref, q_ref, k_ref, v_ref, u_ref):
    xb = x_ref[...].astype(BF16)
    c, sa, sb = c_ref[...], sa_ref[...], sb_ref[...]
    hq = _nn(xb, w_ref[:, 0:QW])
    for s in range(QW // LANES):
        q_ref[:, LANES * s:LANES * (s + 1)] = _rope(hq[:, LANES * s:LANES * (s + 1)], c, sa, sb)
    hkv = _nn(xb, w_ref[:, QW:QW + 2 * KVW])
    k_ref[...] = _rope(hkv[:, 0:KVW], c, sa, sb)
    v_ref[...] = hkv[:, KVW:2 * KVW]
    o = QW + 2 * KVW
    ga = _nn(xb, w_ref[:, o:o + B_CH])
    gb = _nn(xb, w_ref[:, o + B_CH:o + 2 * B_CH])
    u_ref[...] = ga * jax.nn.sigmoid(gb)


def _inproj_odd_body(x_ref, w_ref, c_ref, sa_ref, sb_ref, q_ref, k_ref, v_ref, d_ref):
    xb = x_ref[...].astype(BF16)
    c, sa, sb = c_ref[...], sa_ref[...], sb_ref[...]
    hq = _nn(xb, w_ref[:, 0:QW])
    for s in range(QW // LANES):
        q_ref[:, LANES * s:LANES * (s + 1)] = _rope(hq[:, LANES * s:LANES * (s + 1)], c, sa, sb)
    hkv = _nn(xb, w_ref[:, QW:QW + 2 * KVW])
    k_ref[...] = _rope(hkv[:, 0:KVW], c, sa, sb)
    v_ref[...] = hkv[:, KVW:2 * KVW]
    o = QW + 2 * KVW
    d_ref[:, 0:512] = _silu(_nn(xb, w_ref[:, o:o + 512]))
    d_ref[:, 512:1536] = _nn(xb, w_ref[:, o + 512:o + 1536])
    d_ref[:, 1536:2048] = _silu(_nn(xb, w_ref[:, o + 1536:o + 2048]))


def _inproj(x2, w_bf, layer, tabs, even, tm):
    m = x2.shape[0]
    ntab = tabs[0].shape[0] // tm
    row = lambda i: (i, 0)
    tab = lambda i: (i % ntab, 0)
    wide = B_CH if even else 2048
    return pl.pallas_call(
        _inproj_even_body if even else _inproj_odd_body,
        grid=(m // tm,),
        in_specs=[pl.BlockSpec((tm, D_MODEL), row), _layer_spec(w_bf, layer),
                  pl.BlockSpec((tm, LANES), tab), pl.BlockSpec((tm, LANES), tab), pl.BlockSpec((tm, LANES), tab)],
        out_specs=[pl.BlockSpec((tm, QW), row), pl.BlockSpec((tm, KVW), row), pl.BlockSpec((tm, KVW), row),
                   pl.BlockSpec((tm, wide), row)],
        out_shape=[jax.ShapeDtypeStruct((m, QW), F32), jax.ShapeDtypeStruct((m, KVW), F32),
                   jax.ShapeDtypeStruct((m, KVW), F32), jax.ShapeDtypeStruct((m, wide), F32)],
        compiler_params=_params(("parallel",)),
        name="inproj_even" if even else "inproj_odd",
    )(x2, w_bf, *tabs)


def _queries_t(q):
    return jnp.concatenate([q[:, LANES * s:LANES * (s + 1)].T for s in range(QW // LANES)], 0)


def _stack_heads_t(qt, j):
    t = qt.shape[1]
    z = jnp.zeros((HEAD_DIM, t), F32)
    blocks = []
    for g in range(GROUP):
        h = GROUP * j + g
        hd = qt[HEAD_DIM * h:HEAD_DIM * (h + 1), :]
        blocks.append(jnp.concatenate([hd, z] if j == 0 else [z, hd], 0))
    return jnp.concatenate(blocks, 1)


def _unstack_heads(outs, t):
    lane = lax.broadcasted_iota(jnp.int32, (t, LANES), 1)
    slabs = []
    for p in range(N_HEADS // 2):
        halves = []
        for hh in range(2):
            h = 2 * p + hh
            j, g = h // GROUP, h % GROUP
            blk = outs[j][g * t:(g + 1) * t]
            if j != hh:
                blk = pltpu.roll(blk, HEAD_DIM, 1)
            halves.append(blk)
        slabs.append(jnp.where(lane < HEAD_DIM, halves[0], halves[1]))
    return jnp.concatenate(slabs, 1)


def _untranspose(o_t, t):
    return jnp.concatenate([o_t[:, t * g:t * (g + 1)].T for g in range(GROUP)], 0)


def _moba_prompt_body(q_ref, k_ref, v_ref, o_ref, km_ref, vt_ref, sel_ref, *, nb):
    c = pl.program_id(1)
    blk = c // (MOBA_BLOCK // Q_CHUNK)
    qs0 = c * Q_CHUNK
    cols = N_HEADS * Q_CHUNK
    nbp = km_ref.shape[0]

    @pl.when(c == 0)
    def _():
        km_ref[...] = jnp.concatenate(
            [jnp.mean(k_ref[n * MOBA_BLOCK:(n + 1) * MOBA_BLOCK, :], axis=0, keepdims=True) for n in range(nb)]
            + [jnp.zeros((nbp - nb, LANES), F32)] * (nbp > nb), 0)
        for n in range(nb):
            vt_ref[n] = v_ref[n * MOBA_BLOCK:(n + 1) * MOBA_BLOCK, :].T.astype(BF16)

    qt = _queries_t(q_ref[...] * SCALE)
    blk_row = lax.broadcasted_iota(jnp.int32, (nbp, cols), 0)
    kidx = lax.broadcasted_iota(jnp.int32, (MOBA_BLOCK, cols), 0)
    qpos = qs0 + lax.broadcasted_iota(jnp.int32, (MOBA_BLOCK, cols), 1) % Q_CHUNK
    qs_t = jnp.concatenate([_stack_heads_t(qt, j) for j in range(N_KV)], 1)
    qsb = qs_t.astype(BF16)
    gate = jnp.where(blk_row < blk, _nn(km_ref[...], qs_t, HIGHEST), -jnp.inf)
    sel_rows = []
    for n in range(nb):
        gn = gate[n:n + 1, :]
        beats = jnp.where((gate > gn) | ((gate == gn) & (blk_row < n)), 1.0, 0.0)
        cnt = jnp.sum(beats, axis=0, keepdims=True)
        sel_rows.append(jnp.where((cnt < MOBA_TOPK) & (n < blk), 1.0, 0.0))
    sel_ref[...] = jnp.concatenate(sel_rows + [jnp.zeros((nbp - nb, cols), F32)] * (nbp > nb), 0)

    def scores(n):
        off = pl.multiple_of(n * MOBA_BLOCK, MOBA_BLOCK)
        return _nn(k_ref[pl.ds(off, MOBA_BLOCK), :].astype(BF16), qsb), vt_ref[n]

    s, vt = scores(blk)
    s = jnp.where(kidx + blk * MOBA_BLOCK <= qpos, s, NEG)
    m = jnp.max(s, axis=0, keepdims=True)
    p = jnp.exp(s - m)
    l = jnp.sum(p, axis=0, keepdims=True)
    acc = _nn(vt, p.astype(BF16))

    def past(i, carry):
        m, l, acc = carry
        n0 = 2 * i
        n1 = n0 + 1
        s0, vt0 = scores(n0)
        s1, vt1 = scores(n1)
        s0 = jnp.where(sel_ref[pl.ds(n0, 1), :] > 0.5, s0, NEG)
        s1 = jnp.where(sel_ref[pl.ds(n1, 1), :] > 0.5, s1, NEG)
        m_new = jnp.maximum(m, jnp.maximum(jnp.max(s0, axis=0, keepdims=True), jnp.max(s1, axis=0, keepdims=True)))
        a = jnp.exp(m - m_new)
        p0 = jnp.exp(s0 - m_new)
        p1 = jnp.exp(s1 - m_new)
        l_new = a * l + jnp.sum(p0, axis=0, keepdims=True) + jnp.sum(p1, axis=0, keepdims=True)
        return m_new, l_new, a * acc + _nn(vt0, p0.astype(BF16)) + _nn(vt1, p1.astype(BF16))

    m, l, acc = lax.fori_loop(0, (blk + 1) // 2, past, (m, l, acc))
    o_t = acc / l
    half = GROUP * Q_CHUNK
    o_ref[...] = _unstack_heads([_untranspose(o_t[:, half * j:half * (j + 1)], Q_CHUNK) for j in range(N_KV)], Q_CHUNK)


def _moba_prompt(q, k, v):
    b, l, _ = q.shape
    nb = l // MOBA_BLOCK
    nbp = -(-nb // 8) * 8
    return pl.pallas_call(
        functools.partial(_moba_prompt_body, nb=nb),
        grid=(b, l // Q_CHUNK),
        in_specs=[pl.BlockSpec((None, Q_CHUNK, QW), lambda i, c: (i, c, 0)),
                  pl.BlockSpec((None, l, KVW), lambda i, c: (i, 0, 0)),
                  pl.BlockSpec((None, l, KVW), lambda i, c: (i, 0, 0))],
        out_specs=pl.BlockSpec((None, Q_CHUNK, QW), lambda i, c: (i, c, 0)),
        out_shape=jax.ShapeDtypeStruct((b, l, QW), F32),
        scratch_shapes=[pltpu.VMEM((nbp, LANES), F32), pltpu.VMEM((nb, KVW, MOBA_BLOCK), BF16),
                        pltpu.VMEM((nbp, N_HEADS * Q_CHUNK), F32)],
        compiler_params=_params(("parallel", "arbitrary")),
        name="moba_prompt",
    )(q, k, v)


def _stack_decode(q):
    s = q.shape[0]
    lane = lax.broadcasted_iota(jnp.int32, (s, LANES), 1)
    parts = []
    for h in range(N_HEADS):
        j = h // GROUP
        slab = q[:, LANES * (h // 2):LANES * (h // 2 + 1)]
        if h % 2 != j:
            slab = pltpu.roll(slab, HEAD_DIM, 1)
        keep = (lane < HEAD_DIM) if j == 0 else (lane >= HEAD_DIM)
        parts.append(jnp.where(keep, slab, 0.0))
    return jnp.concatenate(parts, 0)


def _unstack_decode(o, s):
    lane = lax.broadcasted_iota(jnp.int32, (s, LANES), 1)
    slabs = []
    for p in range(N_HEADS // 2):
        halves = []
        for hh in range(2):
            h = 2 * p + hh
            blk = o[h * s:(h + 1) * s]
            if h // GROUP != hh:
                blk = pltpu.roll(blk, HEAD_DIM, 1)
            halves.append(blk)
        slabs.append(jnp.where(lane < HEAD_DIM, halves[0], halves[1]))
    return jnp.concatenate(slabs, 1)


def _pad_rows(x, n):
    return jnp.concatenate([x, jnp.zeros((n - x.shape[0], x.shape[1]), x.dtype)], 0)


def _kv_transposed(x):
    lead = x.shape[:-3]
    n = len(lead)
    t = x.shape[-3]
    return jnp.transpose(x, tuple(range(n)) + (n + 1, n + 2, n)).reshape(lead + (KVW, t))


def _kv_untransposed(xt):
    lead = xt.shape[:-2]
    n = len(lead)
    t = xt.shape[-1]
    return jnp.transpose(xt.reshape(lead + (N_KV, HEAD_DIM, t)), tuple(range(n)) + (n + 2, n, n + 1))


def _moba_sample_body(pt_ref, q_ref, kn_ref, vn_ref, kpool_ref, vpool_ref, o_ref, kbuf_ref, vbuf_ref, kb_ref, s_ref,
                      sem_ref, *, nb, s_len, layer, nbatch):
    ppb = MOBA_BLOCK // PAGE_SIZE
    assert ppb == 2 and 2 * N_HEADS * s_len == LANES
    npages = nb * ppb
    rows = N_HEADS * s_len
    nbp = -(-nb // 8) * 8
    b = pl.program_id(0)
    slot = b % 2
    nslot = 1 - slot
    nxt = jnp.minimum(b + 1, nbatch - 1)

    def k_copy(bi, p, sl):
        return pltpu.make_async_copy(kpool_ref.at[layer, pt_ref[bi, p]], kbuf_ref.at[sl, p], sem_ref.at[sl, 0])

    def v_copy(bi, p, sl):
        return pltpu.make_async_copy(vpool_ref.at[layer, pt_ref[bi, p]], vbuf_ref.at[sl, p], sem_ref.at[sl, 1])

    @pl.when(b == 0)
    def _():
        for p in range(npages):
            k_copy(0, p, 0).start()
            v_copy(0, p, 0).start()

    for p in range(npages):
        k_copy(b, p, slot).wait()
        v_copy(b, p, slot).wait()

    qa = _stack_decode(q_ref[...] * SCALE)
    qt = jnp.concatenate([qa, qa], 0).T
    lane = lax.broadcasted_iota(jnp.int32, (LANES, LANES), 1)
    lo = lane < rows
    qt_lo = jnp.where(lo, qt, 0.0).astype(BF16)
    qt_hi = jnp.where(lo, 0.0, qt).astype(BF16)
    sums = []
    for p in range(npages):
        k_copy(nxt, p, nslot).start()
        v_copy(nxt, p, nslot).start()
        kpg = kbuf_ref[slot, p].T
        kb_ref[p] = kpg.astype(BF16)
        sums.append(jnp.sum(kpg, axis=0, keepdims=True))
    km = jnp.concatenate([sum(sums[ppb * n:ppb * (n + 1)]) * (1.0 / MOBA_BLOCK) for n in range(nb)]
                         + [jnp.zeros((nbp - nb, LANES), F32)] * (nbp > nb), 0)
    blk_row = lax.broadcasted_iota(jnp.int32, (nbp, LANES), 0)
    gate = jnp.where(blk_row < nb, _nn(km, qt, HIGHEST), -jnp.inf)
    sel = []
    for n in range(nb):
        gn = gate[n:n + 1, :]
        beats = jnp.where((gate > gn) | ((gate == gn) & (blk_row < n)), 1.0, 0.0)
        sel.append(jnp.sum(beats, axis=0, keepdims=True) < MOBA_TOPK)
    kidx = lax.broadcasted_iota(jnp.int32, (LANES, LANES), 0)
    tok = lane % s_len
    s_new = jnp.where((kidx <= tok) & (kidx < s_len) & lo, _nn(_pad_rows(kn_ref[...], LANES).astype(BF16), qt_lo), NEG)
    m = jnp.max(s_new, axis=0, keepdims=True)
    for n in range(nb):
        s = jnp.where(sel[n], _nn(kb_ref[ppb * n], qt_lo) + _nn(kb_ref[ppb * n + 1], qt_hi), NEG)
        s_ref[n] = s
        m = jnp.maximum(m, jnp.max(s, axis=0, keepdims=True))
    m = jnp.maximum(m, pltpu.roll(m, rows, 1))
    p_new = jnp.exp(s_new - m)
    l = jnp.sum(p_new, axis=0, keepdims=True)
    acc_lo = _nn(_pad_rows(vn_ref[...], LANES).T.astype(BF16), p_new.astype(BF16))
    acc_hi = jnp.zeros((KVW, LANES), F32)
    for n in range(nb):
        pp = jnp.exp(s_ref[n] - m)
        l = l + jnp.sum(pp, axis=0, keepdims=True)
        ppb16 = pp.astype(BF16)
        acc_lo = acc_lo + _nn(vbuf_ref[slot, ppb * n].astype(BF16), ppb16)
        acc_hi = acc_hi + _nn(vbuf_ref[slot, ppb * n + 1].astype(BF16), ppb16)
    acc = jnp.where(lo, acc_lo, acc_hi)
    o_t = (acc + pltpu.roll(acc, rows, 1)) / (l + pltpu.roll(l, rows, 1))
    o_ref[...] = _unstack_decode(o_t.T[0:rows], s_len)

    @pl.when(b == nbatch - 1)
    def _():
        for p in range(npages):
            k_copy(nxt, p, nslot).wait()
            v_copy(nxt, p, nslot).wait()


def _moba_sample(q, k_new, v_new, k_pool_t, v_pool_t, page_table, layer):
    b, s_len, _ = q.shape
    npages = page_table.shape[1]
    assert (npages * PAGE_SIZE) % MOBA_BLOCK == 0 and N_HEADS * s_len <= LANES
    nb = npages * PAGE_SIZE // MOBA_BLOCK
    assert 0 < nb <= LANES

    tok = lambda i, pt: (i, 0, 0)
    page_buf = pltpu.VMEM((2, npages, KVW, PAGE_SIZE), F32)
    grid_spec = pltpu.PrefetchScalarGridSpec(
        num_scalar_prefetch=1,
        grid=(b,),
        in_specs=[pl.BlockSpec((None, s_len, QW), tok), pl.BlockSpec((None, s_len, KVW), tok),
                  pl.BlockSpec((None, s_len, KVW), tok),
                  pl.BlockSpec(memory_space=pl.ANY), pl.BlockSpec(memory_space=pl.ANY)],
        out_specs=pl.BlockSpec((None, s_len, QW), tok),
        scratch_shapes=[page_buf, page_buf, pltpu.VMEM((npages, PAGE_SIZE, KVW), BF16),
                        pltpu.VMEM((nb, PAGE_SIZE, LANES), F32), pltpu.SemaphoreType.DMA((2, 2))],
    )
    return pl.pallas_call(
        functools.partial(_moba_sample_body, nb=nb, s_len=s_len, layer=layer, nbatch=b),
        grid_spec=grid_spec,
        out_shape=jax.ShapeDtypeStruct((b, s_len, QW), F32),
        compiler_params=_params(("arbitrary",)),
        name="moba_sample",
    )(page_table, q, k_new, v_new, k_pool_t, v_pool_t)


def _swa_prompt_body(q_ref, kp_ref, kc_ref, vp_ref, vc_ref, sink_ref, o_ref):
    t = pl.program_id(1)
    cols = N_HEADS * WINDOW
    qt = _queries_t(q_ref[...] * SCALE)
    kk = jnp.concatenate([kp_ref[...], kc_ref[...]], 0).astype(BF16)
    vv_t = jnp.concatenate([vp_ref[...].T, vc_ref[...].T], 1).astype(BF16)
    kidx = lax.broadcasted_iota(jnp.int32, (2 * WINDOW, cols), 0)
    tq = lax.broadcasted_iota(jnp.int32, (2 * WINDOW, cols), 1) % WINDOW
    lo = jnp.where(t == 0, WINDOW, 0)
    valid = (kidx >= tq) & (kidx <= tq + WINDOW) & (kidx >= lo)
    qs_t = jnp.concatenate([_stack_heads_t(qt, j) for j in range(N_KV)], 1)
    s = jnp.where(valid, _nn(kk, qs_t.astype(BF16)), NEG)
    sink = sink_ref[...]
    m = jnp.maximum(jnp.max(s, axis=0, keepdims=True), sink)
    p = jnp.exp(s - m)
    den = jnp.sum(p, axis=0, keepdims=True) + jnp.exp(sink - m)
    o_t = _nn(vv_t, p.astype(BF16)) / den
    half = GROUP * WINDOW
    o_ref[...] = _unstack_heads([_untranspose(o_t[:, half * j:half * (j + 1)], WINDOW) for j in range(N_KV)], WINDOW)


def _swa_prompt(q, k, v, sink):
    b, l, _ = q.shape
    sink_cols = jnp.repeat(sink.astype(F32), WINDOW)[None, :]
    cur = lambda i, t: (i, t, 0)
    prev = lambda i, t: (i, jnp.maximum(t - 1, 0), 0)
    return pl.pallas_call(
        _swa_prompt_body,
        grid=(b, l // WINDOW),
        in_specs=[pl.BlockSpec((None, WINDOW, QW), cur),
                  pl.BlockSpec((None, WINDOW, KVW), prev), pl.BlockSpec((None, WINDOW, KVW), cur),
                  pl.BlockSpec((None, WINDOW, KVW), prev), pl.BlockSpec((None, WINDOW, KVW), cur),
                  pl.BlockSpec((1, N_HEADS * WINDOW), lambda i, t: (0, 0))],
        out_specs=pl.BlockSpec((None, WINDOW, QW), cur),
        out_shape=jax.ShapeDtypeStruct((b, l, QW), F32),
        compiler_params=_params(("parallel", "arbitrary")),
        name="swa_prompt",
    )(q, k, k, v, v, sink_cols)


def _swa_sample_body(q_ref, kn_ref, vn_ref, wk_ref, wv_ref, sink_ref, o_ref, nk_ref, nv_ref, *, bt, s_len):
    rows = N_HEADS * s_len
    tok = lax.broadcasted_iota(jnp.int32, (rows, LANES), 0) % s_len
    lane = lax.broadcasted_iota(jnp.int32, (rows, LANES), 1)
    lane_w = lax.broadcasted_iota(jnp.int32, (KVW, WINDOW), 1)
    sink = sink_ref[...]

    def one(i, carry):
        qab = _stack_decode(q_ref[i] * SCALE).astype(BF16)
        kn = _pad_rows(kn_ref[i], LANES)
        vn = _pad_rows(vn_ref[i], LANES)
        wk_t = wk_ref[i]
        wv_t = wv_ref[i]
        s_win = jnp.where(lane >= tok, _nn(qab, wk_t.astype(BF16)), NEG)
        s_new = jnp.where((lane <= tok) & (lane < s_len), _nt(qab, kn.astype(BF16)), NEG)
        m = jnp.maximum(jnp.maximum(jnp.max(s_win, axis=1, keepdims=True), jnp.max(s_new, axis=1, keepdims=True)), sink)
        p_win = jnp.exp(s_win - m)
        p_new = jnp.exp(s_new - m)
        den = jnp.sum(p_win, axis=1, keepdims=True) + jnp.sum(p_new, axis=1, keepdims=True) + jnp.exp(sink - m)
        acc = _nt(p_win.astype(BF16), wv_t.astype(BF16)) + _nn(p_new.astype(BF16), vn.astype(BF16))
        o_ref[i] = _unstack_decode(acc / den, s_len)
        keep = lane_w < WINDOW - s_len
        nk_ref[i] = jnp.where(keep, pltpu.roll(wk_t, WINDOW - s_len, 1), pltpu.roll(kn.T, WINDOW - s_len, 1))
        nv_ref[i] = jnp.where(keep, pltpu.roll(wv_t, WINDOW - s_len, 1), pltpu.roll(vn.T, WINDOW - s_len, 1))
        return carry

    lax.fori_loop(0, bt, one, 0)


def _swa_sample(q, k_new, v_new, win_k_t, win_v_t, layer, sink, bt=16):
    b, s_len, _ = q.shape
    bt = min(bt, b)
    assert win_k_t.shape[3] == WINDOW == LANES and b % bt == 0 and s_len <= WINDOW
    sink_rows = jnp.repeat(sink.astype(F32), s_len)[:, None]
    blk = lambda i: (i, 0, 0)
    win = pl.BlockSpec((bt, KVW, WINDOW), blk)
    win_in = pl.BlockSpec((None, bt, KVW, WINDOW), lambda i: (layer, i, 0, 0))
    return pl.pallas_call(
        functools.partial(_swa_sample_body, bt=bt, s_len=s_len),
        grid=(b // bt,),
        in_specs=[pl.BlockSpec((bt, s_len, QW), blk), pl.BlockSpec((bt, s_len, KVW), blk),
                  pl.BlockSpec((bt, s_len, KVW), blk), win_in, win_in,
                  pl.BlockSpec((N_HEADS * s_len, 1), lambda i: (0, 0))],
        out_specs=[pl.BlockSpec((bt, s_len, QW), blk), win, win],
        out_shape=[jax.ShapeDtypeStruct((b, s_len, QW), F32), jax.ShapeDtypeStruct((b, KVW, WINDOW), F32),
                   jax.ShapeDtypeStruct((b, KVW, WINDOW), F32)],
        compiler_params=_params(("parallel",)),
        name="swa_sample",
    )(q, k_new, v_new, win_k_t, win_v_t, sink_rows)


def _conv_prompt_body(prev_ref, cur_ref, w_ref, b_ref, g_ref, bb_ref, o_ref, ext_ref, sh_ref, *, tm, halo):
    t = pl.program_id(1)
    ext_ref[0:halo, :] = jnp.where(t == 0, 0.0, prev_ref[...])
    ext_ref[halo:halo + tm, :] = cur_ref[...]
    kw = w_ref.shape[0]
    base = halo - (kw - 1)
    sub = 8
    for lc in range(B_CH // LANES):
        ln = slice(LANES * lc, LANES * (lc + 1))
        acc = jnp.zeros((tm, LANES), F32) + b_ref[:, ln]
        for r in range(sub):
            offs = [o for o in range(base, base + kw) if o % sub == r]
            if not offs:
                continue
            n = tm + max(offs) - r
            if r:
                sh_ref[0:n, :] = ext_ref[pl.ds(r, n), ln]
            src = sh_ref if r else ext_ref.at[:, ln]
            for o in offs:
                acc = acc + w_ref[o - base:o - base + 1, ln] * src[pl.ds(o - r, tm), :]
        o_ref[:, ln] = acc
    o_ref[...] = _silu(_ln(o_ref[...], g_ref[...], bb_ref[...]))


def _conv_prompt(u, w, b, g, bb, tm=256, halo=32):
    bsz, l, _ = u.shape
    r = tm // halo
    vec = lambda i, t: (0, 0)
    return pl.pallas_call(
        functools.partial(_conv_prompt_body, tm=tm, halo=halo),
        grid=(bsz, l // tm),
        in_specs=[pl.BlockSpec((None, halo, B_CH), lambda i, t: (i, jnp.maximum(t * r - 1, 0), 0)),
                  pl.BlockSpec((None, tm, B_CH), lambda i, t: (i, t, 0)),
                  pl.BlockSpec(w.shape, vec), pl.BlockSpec((1, B_CH), vec), pl.BlockSpec((1, B_CH), vec),
                  pl.BlockSpec((1, B_CH), vec)],
        out_specs=pl.BlockSpec((None, tm, B_CH), lambda i, t: (i, t, 0)),
        out_shape=jax.ShapeDtypeStruct((bsz, l, B_CH), F32),
        scratch_shapes=[pltpu.VMEM((tm + halo, B_CH), F32), pltpu.VMEM((tm + halo, LANES), F32)],
        compiler_params=_params(("parallel", "arbitrary")),
        name="conv_prompt",
    )(u, u, w, b[None], g[None], bb[None])


def _conv_sample_body(up_ref, w_ref, b_ref, g_ref, bb_ref, o_ref, *, bt, s_len):
    kw = w_ref.shape[0]
    acc = jnp.zeros((bt, s_len, B_CH), F32) + b_ref[...]
    for j in range(kw):
        acc = acc + w_ref[j:j + 1, :] * up_ref[:, pl.ds(j, s_len), :]
    o_ref[...] = _silu(_ln(acc, g_ref[...], bb_ref[...]))


def _conv_sample(up, w, b, g, bb, bt=32):
    bsz, rows, _ = up.shape
    bt = min(bt, bsz)
    s_len = rows - (w.shape[0] - 1)
    vec = lambda i: (0, 0)
    return pl.pallas_call(
        functools.partial(_conv_sample_body, bt=bt, s_len=s_len),
        grid=(bsz // bt,),
        in_specs=[pl.BlockSpec((bt, rows, B_CH), lambda i: (i, 0, 0)),
                  pl.BlockSpec(w.shape, vec), pl.BlockSpec((1, B_CH), vec), pl.BlockSpec((1, B_CH), vec),
                  pl.BlockSpec((1, B_CH), vec)],
        out_specs=pl.BlockSpec((bt, s_len, B_CH), lambda i: (i, 0, 0)),
        out_shape=jax.ShapeDtypeStruct((bsz, s_len, B_CH), F32),
        compiler_params=_params(("parallel",)),
        name="conv_sample",
    )(up, w, b[None], g[None], bb[None])


def _hgrn_tile(hq, df, v, lb, ch):
    n = LANES // ch
    loglb, log1mlb, onemlb = lb[0:1, :], lb[1:2, :], lb[2:3, :]
    e = jnp.exp(-jnp.abs(df))
    ls = jnp.minimum(df, 0.0) - jnp.log1p(e)
    b_ = log1mlb + ls
    mx = jnp.maximum(loglb, b_)
    logf = mx + jnp.log(jnp.exp(loglb - mx) + jnp.exp(b_ - mx))
    hk = onemlb * (jnp.where(df >= 0.0, e, 1.0) / (1.0 + e))
    row = lax.broadcasted_iota(jnp.int32, (LANES, LANES), 0)
    colv = lax.broadcasted_iota(jnp.int32, (LANES, LANES), 1)
    r = row % ch
    g = logf
    sh = 1
    while sh < ch:
        g = g + jnp.where(r >= sh, pltpu.roll(g, sh, 0), 0.0)
        sh *= 2
    qa, kb, kd, qe, dec = [], [], [], [], []
    for c in range(n):
        sl = slice(ch * c, ch * (c + 1))
        gc = g[sl]
        gm = gc[ch // 2 - 1:ch // 2]
        gl = gc[ch - 1:ch]
        qa.append(hq[sl] * jnp.exp(gc - gm))
        kb.append(hk[sl] * jnp.exp(gm - gc))
        kd.append(hk[sl] * jnp.exp(gl - gc))
        qe.append((hq[sl] * jnp.exp(gc)).astype(BF16))
        dec.append(jnp.exp(gl))
    att = _nt(jnp.concatenate(qa, 0).astype(BF16), jnp.concatenate(kb, 0).astype(BF16))
    att = jnp.where((row // ch == colv // ch) & (colv <= row), att, 0.0)
    vb = v.astype(BF16)
    o_intra = _nn(att.astype(BF16), vb)
    vt = v.T
    kdb = jnp.concatenate(kd, 0).astype(BF16)
    ut = [_nn(jnp.where(colv // ch == c, vt, 0.0).astype(BF16), kdb) for c in range(n)]
    return o_intra, ut, qe, dec


def _hgrn_finish(o, sg, gn):
    return o * lax.rsqrt(jnp.mean(o * o, axis=-1, keepdims=True) + LN_EPS) * gn * sg


def _hgrn_prompt_body(hq_ref, df_ref, di_ref, sg_ref, lb_ref, gn_ref, o_ref, s_ref, *, ngroups, ch, nh):
    def group(gi, sts):
        off = pl.multiple_of(gi * LANES, LANES)
        rows = pl.ds(off, LANES)
        new = []
        for hh in range(nh):
            ln = slice(LANES * hh, LANES * (hh + 1))
            st = sts[hh]
            o_intra, ut, qe, dec = _hgrn_tile(hq_ref[rows, ln], df_ref[rows, ln], di_ref[rows, ln], lb_ref[:, ln], ch)
            o_inter = []
            for c in range(LANES // ch):
                o_inter.append(_nt(qe[c], st.astype(BF16)))
                st = dec[c] * st + ut[c]
            o = o_intra + jnp.concatenate(o_inter, 0)
            o_ref[rows, ln] = _hgrn_finish(o, sg_ref[rows, ln], gn_ref[...])
            new.append(st)
        return tuple(new)

    zero = jnp.zeros((LANES, LANES), F32)
    sts = lax.fori_loop(0, ngroups, group, (zero,) * nh)
    for hh in range(nh):
        s_ref[hh] = sts[hh].T


def _hgrn_prompt(d4, lbt, gn, ch=16, nh=2):
    b, l, _ = d4.shape
    hp = D_HEADS // nh
    seg = lambda s: pl.BlockSpec((None, l, nh * LANES), lambda i, h: (i, 0, hp * s + h))
    return pl.pallas_call(
        functools.partial(_hgrn_prompt_body, ngroups=l // LANES, ch=ch, nh=nh),
        grid=(b, hp),
        in_specs=[seg(0), seg(1), seg(2), seg(3),
                  pl.BlockSpec((3, nh * LANES), lambda i, h: (0, h)), pl.BlockSpec((1, LANES), lambda i, h: (0, 0))],
        out_specs=[pl.BlockSpec((None, l, nh * LANES), lambda i, h: (i, 0, h)),
                   pl.BlockSpec((None, nh, D_DK, LANES), lambda i, h: (i, h, 0, 0))],
        out_shape=[jax.ShapeDtypeStruct((b, l, D_HEADS * LANES), F32),
                   jax.ShapeDtypeStruct((b, D_HEADS, D_DK, LANES), F32)],
        compiler_params=_params(("parallel", "arbitrary")),
        name="hgrn_prompt",
    )(d4, d4, d4, d4, lbt, gn[None])


def _hgrn_sample_body(hq_ref, df_ref, di_ref, sg_ref, lb_ref, gn_ref, s0_ref, o_ref, s_ref, *, ch):
    o_intra, ut, qe, dec = _hgrn_tile(hq_ref[...], df_ref[...], di_ref[...], lb_ref[...], ch)
    o_inter = []
    for c in range(LANES // ch):
        st = s0_ref[c].T
        o_inter.append(_nt(qe[c], st.astype(BF16)))
        s_ref[c] = (dec[c] * st + ut[c]).T
    o = o_intra + jnp.concatenate(o_inter, 0)
    o_ref[...] = _hgrn_finish(o, sg_ref[...], gn_ref[...])


def _hgrn_sample(d4, lbt, gn, s0, layer):
    m = d4.shape[0]
    b = s0.shape[1]
    s_len = m // b
    assert LANES % s_len == 0 and s_len & (s_len - 1) == 0 and m % LANES == 0
    bt = LANES // s_len
    seg = lambda s: pl.BlockSpec((LANES, LANES), lambda i, h: (i, D_HEADS * s + h))
    st_in = pl.BlockSpec((None, bt, None, D_DK, LANES), lambda i, h: (layer, i, h, 0, 0))
    st_spec = pl.BlockSpec((bt, None, D_DK, LANES), lambda i, h: (i, h, 0, 0))
    return pl.pallas_call(
        functools.partial(_hgrn_sample_body, ch=s_len),
        grid=(m // LANES, D_HEADS),
        in_specs=[seg(0), seg(1), seg(2), seg(3),
                  pl.BlockSpec((3, LANES), lambda i, h: (0, h)), pl.BlockSpec((1, LANES), lambda i, h: (0, 0)), st_in],
        out_specs=[pl.BlockSpec((LANES, LANES), lambda i, h: (i, h)), st_spec],
        out_shape=[jax.ShapeDtypeStruct((m, D_HEADS * LANES), F32), jax.ShapeDtypeStruct(s0.shape[1:], F32)],
        compiler_params=_params(("parallel", "arbitrary")),
        name="hgrn_sample",
    )(d4, d4, d4, d4, lbt, gn[None], s0)


def _outproj_body(a_ref, c_ref, x_ref, w_ref, g_ref, b_ref, o_ref):
    wa = a_ref.shape[1]
    y = _nn(a_ref[...].astype(BF16), w_ref[0:wa, :]) + _nn(c_ref[...].astype(BF16), w_ref[wa:, :])
    o_ref[...] = _ln(ALPHA * x_ref[...] + y, g_ref[...], b_ref[...])


def _outproj_ln(a, c, x2, w_bf, layer, g, b, tm):
    m = x2.shape[0]
    row = lambda i: (i, 0)
    vec = lambda i: (0, 0)
    return pl.pallas_call(
        _outproj_body,
        grid=(m // tm,),
        in_specs=[pl.BlockSpec((tm, a.shape[1]), row), pl.BlockSpec((tm, c.shape[1]), row),
                  pl.BlockSpec((tm, D_MODEL), row), _layer_spec(w_bf, layer),
                  pl.BlockSpec((1, D_MODEL), vec), pl.BlockSpec((1, D_MODEL), vec)],
        out_specs=pl.BlockSpec((tm, D_MODEL), row),
        out_shape=jax.ShapeDtypeStruct((m, D_MODEL), F32),
        compiler_params=_params(("parallel",)),
        name="outproj_ln",
    )(a, c, x2, w_bf, g[None], b[None])


def _ffn_body(x_ref, hist_ref, wg_ref, wu_ref, wd_ref, wc_ref, bc_ref, g_ref, b_ref,
              o_ref, st_ref, *carry, tm, seg, fc):
    x = x_ref[...]
    xb = x.astype(BF16)
    row = lax.broadcasted_iota(jnp.int32, (tm, fc), 0)
    r = row % seg
    if seg == tm:
        carry_ref, = carry

        @pl.when(pl.program_id(1) == 0)
        def _():
            carry_ref[...] = hist_ref[...]

    acc = jnp.zeros((tm, D_MODEL), F32)
    for c in range(D_FF // fc):
        cols = slice(fc * c, fc * (c + 1))
        gp = _nn(xb, wg_ref[:, cols])
        if seg == tm:
            h0 = carry_ref[0:1, cols]
            h1 = carry_ref[1:2, cols]
            p1 = jnp.where(row == 0, h1, pltpu.roll(gp, 1, 0))
            p2 = jnp.where(row == 0, h0, jnp.where(row == 1, h1, pltpu.roll(gp, 2, 0)))
            carry_ref[:, cols] = gp[tm - 2:tm]
            st_ref[:, cols] = gp[tm - 2:tm]
        else:
            hp = hist_ref[:, cols]
            p1 = jnp.where(r == 0, pltpu.roll(hp, (tm - (seg - 1)) % tm, 0), pltpu.roll(gp, 1, 0))
            p2 = jnp.where(r < 2, pltpu.roll(hp, (tm - (seg - 2)) % tm, 0), pltpu.roll(gp, 2, 0))
            st_ref[:, cols] = gp
        cv = wc_ref[0:1, cols] * p2 + wc_ref[1:2, cols] * p1 + wc_ref[2:3, cols] * gp + bc_ref[:, cols]
        hid = 0.5 * cv * (1.0 + lax.erf(cv * (2.0 ** -0.5))) * _nn(xb, wu_ref[:, cols])
        acc = acc + _nn(hid.astype(BF16), wd_ref[cols, :])
    o_ref[...] = _ln(ALPHA * x + acc, g_ref[...], b_ref[...])


def _ffn_prompt(x, hist, wg, wu, wd, layer, wc, bc, g, b, tm=1024, fc=256):
    bsz, l, _ = x.shape
    tm = min(tm, l)
    assert l % tm == 0
    vec = lambda i, t: (0, 0)
    return pl.pallas_call(
        functools.partial(_ffn_body, tm=tm, seg=tm, fc=fc),
        grid=(bsz, l // tm),
        in_specs=[pl.BlockSpec((None, tm, D_MODEL), lambda i, t: (i, t, 0)),
                  pl.BlockSpec((None, 2, D_FF), lambda i, t: (i, 0, 0)),
                  _layer_spec(wg, layer), _layer_spec(wu, layer), _layer_spec(wd, layer),
                  pl.BlockSpec(wc.shape, vec), pl.BlockSpec((1, D_FF), vec),
                  pl.BlockSpec((1, D_MODEL), vec), pl.BlockSpec((1, D_MODEL), vec)],
        out_specs=[pl.BlockSpec((None, tm, D_MODEL), lambda i, t: (i, t, 0)),
                   pl.BlockSpec((None, 2, D_FF), lambda i, t: (i, 0, 0))],
        out_shape=[jax.ShapeDtypeStruct((bsz, l, D_MODEL), F32), jax.ShapeDtypeStruct((bsz, 2, D_FF), F32)],
        scratch_shapes=[pltpu.VMEM((FFN_CONV_WIDTH - 1, D_FF), F32)],
        compiler_params=_params(("parallel", "arbitrary")),
        name="ffn_prompt",
    )(x, hist, wg, wu, wd, wc, bc[None], g[None], b[None])


def _ffn_sample(x2, hist_rows, s_len, wg, wu, wd, layer, wc, bc, g, b, tm=512, fc=256):
    m = x2.shape[0]
    tm = min(tm, m)
    vec = lambda i: (0, 0)
    row = lambda i: (i, 0)
    return pl.pallas_call(
        functools.partial(_ffn_body, tm=tm, seg=s_len, fc=fc),
        grid=(m // tm,),
        in_specs=[pl.BlockSpec((tm, D_MODEL), row), pl.BlockSpec((tm, D_FF), row),
                  _layer_spec(wg, layer), _layer_spec(wu, layer), _layer_spec(wd, layer),
                  pl.BlockSpec(wc.shape, vec), pl.BlockSpec((1, D_FF), vec),
                  pl.BlockSpec((1, D_MODEL), vec), pl.BlockSpec((1, D_MODEL), vec)],
        out_specs=[pl.BlockSpec((tm, D_MODEL), row), pl.BlockSpec((tm, D_FF), row)],
        out_shape=[jax.ShapeDtypeStruct((m, D_MODEL), F32), jax.ShapeDtypeStruct((m, D_FF), F32)],
        compiler_params=_params(("parallel",)),
        name="ffn_sample",
    )(x2, hist_rows, wg, wu, wd, wc, bc[None], g[None], b[None])


def _lb_tables(lb_param):
    lbs = jnp.cumsum(jax.nn.softmax(lb_param.astype(F32), axis=0), axis=0)
    lbs = lbs - lbs[:1]
    return jnp.stack([jnp.log(lbs), jnp.log1p(-lbs), 1.0 - lbs], axis=1)


def _trunk_prompt(x, prm, lbt):
    b, l, _ = x.shape
    m = b * l
    tm = 512
    tabs = _rope_tables(jnp.arange(l))
    kv_k, kv_v, conv_b, win_k, win_v, hgrn, ffn = [], [], [], [], [], [], []
    zero_hist = jnp.zeros((b, FFN_CONV_WIDTH - 1, D_FF), F32)
    for layer in range(DEPTH):
        i = layer // 2
        x2 = x.reshape(m, D_MODEL)
        if layer % 2 == 0:
            q, k, v, u = _inproj(x2, prm["w_in_even"], i, tabs, True, tm)
            k3, v3, u3 = k.reshape(b, l, KVW), v.reshape(b, l, KVW), u.reshape(b, l, B_CH)
            a = _moba_prompt(q.reshape(b, l, QW), k3, v3)
            cv = _conv_prompt(u3, prm["w_dw_b"][i], prm["b_dw_b"][i], prm["conv_ln_g"][i], prm["conv_ln_b"][i])
            kv_k.append(k3.reshape(b, l, N_KV, HEAD_DIM))
            kv_v.append(v3.reshape(b, l, N_KV, HEAD_DIM))
            conv_b.append(u3[:, l - (B_CONV_WIDTH - 1):])
            x1 = _outproj_ln(a.reshape(m, QW), cv.reshape(m, B_CH), x2, prm["w_out_even"], i,
                             prm["ln1_g"][layer], prm["ln1_b"][layer], tm)
        else:
            q, k, v, d4 = _inproj(x2, prm["w_in_odd"], i, tabs, False, tm)
            k3, v3 = k.reshape(b, l, KVW), v.reshape(b, l, KVW)
            a = _swa_prompt(q.reshape(b, l, QW), k3, v3, prm["sinks"][i])
            o, st = _hgrn_prompt(d4.reshape(b, l, 2048), lbt[i], prm["gnorm_g"][i])
            keep = min(WINDOW, l)
            win_k.append(k3[:, l - keep:].reshape(b, keep, N_KV, HEAD_DIM))
            win_v.append(v3[:, l - keep:].reshape(b, keep, N_KV, HEAD_DIM))
            hgrn.append(st)
            x1 = _outproj_ln(a.reshape(m, QW), o.reshape(m, 512), x2, prm["w_out_odd"], i,
                             prm["ln1_g"][layer], prm["ln1_b"][layer], tm)
        xo, fb = _ffn_prompt(x1.reshape(b, l, D_MODEL), zero_hist, prm["w_ffn_gate"], prm["w_ffn_up"],
                             prm["w_ffn_down"], layer, prm["w_ffn_conv"][layer], prm["b_ffn_conv"][layer],
                             prm["ln2_g"][layer], prm["ln2_b"][layer])
        ffn.append(fb)
        x = xo
    return (x, jnp.stack(kv_k), jnp.stack(kv_v), jnp.stack(conv_b), jnp.stack(win_k), jnp.stack(win_v),
            jnp.stack(hgrn), jnp.stack(ffn))


def _trunk_sample(x, prm, lbt, cache_k, cache_v, page_table, state_conv_b, state_win_k, state_win_v,
                  state_hgrn, state_ffn):
    b, s, _ = x.shape
    m = b * s
    tm = min(512, m)
    past = page_table.shape[1] * PAGE_SIZE
    assert state_win_k.shape[2] == WINDOW and past + s >= WINDOW
    tabs = _rope_tables(jnp.tile(past + jnp.arange(s), tm // s))
    pool_k_t = _kv_transposed(cache_k)
    pool_v_t = _kv_transposed(cache_v)
    win_k_t = _kv_transposed(state_win_k)
    win_v_t = _kv_transposed(state_win_v)
    kv_k, kv_v, conv_b, win_k, win_v, hgrn, ffn = [], [], [], [], [], [], []
    for layer in range(DEPTH):
        i = layer // 2
        x2 = x.reshape(m, D_MODEL)
        if layer % 2 == 0:
            q, k, v, u = _inproj(x2, prm["w_in_even"], i, tabs, True, tm)
            k3, v3 = k.reshape(b, s, KVW), v.reshape(b, s, KVW)
            a = _moba_sample(q.reshape(b, s, QW), k3, v3, pool_k_t, pool_v_t, page_table, i)
            up = jnp.concatenate([state_conv_b[i], u.reshape(b, s, B_CH)], 1)
            cv = _conv_sample(up, prm["w_dw_b"][i], prm["b_dw_b"][i], prm["conv_ln_g"][i], prm["conv_ln_b"][i])
            kv_k.append(k3.reshape(b, s, N_KV, HEAD_DIM))
            kv_v.append(v3.reshape(b, s, N_KV, HEAD_DIM))
            conv_b.append(up[:, up.shape[1] - (B_CONV_WIDTH - 1):])
            x1 = _outproj_ln(a.reshape(m, QW), cv.reshape(m, B_CH), x2, prm["w_out_even"], i,
                             prm["ln1_g"][layer], prm["ln1_b"][layer], tm)
        else:
            q, k, v, d4 = _inproj(x2, prm["w_in_odd"], i, tabs, False, tm)
            k3, v3 = k.reshape(b, s, KVW), v.reshape(b, s, KVW)
            a, nk_t, nv_t = _swa_sample(q.reshape(b, s, QW), k3, v3, win_k_t, win_v_t, i, prm["sinks"][i])
            o, st = _hgrn_sample(d4, lbt[i], prm["gnorm_g"][i], state_hgrn, i)
            win_k.append(_kv_untransposed(nk_t))
            win_v.append(_kv_untransposed(nv_t))
            hgrn.append(st)
            x1 = _outproj_ln(a.reshape(m, QW), o, x2, prm["w_out_odd"], i,
                             prm["ln1_g"][layer], prm["ln1_b"][layer], tm)
        hist_rows = jnp.pad(state_ffn[layer], ((0, 0), (s - (FFN_CONV_WIDTH - 1), 0), (0, 0))).reshape(m, D_FF)
        xo, gp = _ffn_sample(x1, hist_rows, s, prm["w_ffn_gate"], prm["w_ffn_up"],
                             prm["w_ffn_down"], layer, prm["w_ffn_conv"][layer], prm["b_ffn_conv"][layer],
                             prm["ln2_g"][layer], prm["ln2_b"][layer], tm=tm)
        ffn.append(gp.reshape(b, s, D_FF)[:, s - (FFN_CONV_WIDTH - 1):])
        x = xo.reshape(b, s, D_MODEL)
    return (x, jnp.stack(kv_k), jnp.stack(kv_v), jnp.stack(conv_b), jnp.stack(win_k), jnp.stack(win_v),
            jnp.stack(hgrn), jnp.stack(ffn))


def kernel(x_prompt, x_sample, cache_k, cache_v, state_conv_b, state_win_k, state_win_v, state_hgrn, state_ffn,
           page_table, w_in_even, w_dw_b, b_dw_b, conv_ln_g, conv_ln_b, w_out_even, w_in_odd, sinks, lb_param,
           gnorm_g, w_out_odd, ln1_g, ln1_b, ln2_g, ln2_b, w_ffn_gate, w_ffn_up, w_ffn_conv, b_ffn_conv, w_ffn_down):
    prm = dict(w_in_even=w_in_even.astype(BF16), w_dw_b=w_dw_b, b_dw_b=b_dw_b, conv_ln_g=conv_ln_g,
               conv_ln_b=conv_ln_b, w_out_even=w_out_even.astype(BF16), w_in_odd=w_in_odd.astype(BF16), sinks=sinks,
               gnorm_g=gnorm_g, w_out_odd=w_out_odd.astype(BF16), ln1_g=ln1_g, ln1_b=ln1_b, ln2_g=ln2_g, ln2_b=ln2_b,
               w_ffn_gate=w_ffn_gate.astype(BF16), w_ffn_up=w_ffn_up.astype(BF16), w_ffn_conv=w_ffn_conv,
               b_ffn_conv=b_ffn_conv, w_ffn_down=w_ffn_down.astype(BF16))
    lbt = _lb_tables(lb_param)
    yp, kkp, kvp, cbp, wkp, wvp, hgp, ffp = _trunk_prompt(x_prompt, prm, lbt)
    ys, kks, kvs, cbs, wks, wvs, hgs, ffs = _trunk_sample(
        x_sample, prm, lbt, cache_k, cache_v, page_table, state_conv_b, state_win_k, state_win_v, state_hgrn, state_ffn)
    return (yp, ys, kkp, kvp, kks, kvs, cbp, cbs, wkp, wvp, wks, wvs, hgp, hgs, ffp, ffs)
```

```python
import functools

import jax
import jax.numpy as jnp
from jax import lax
from jax.experimental import pallas as pl
from jax.experimental.pallas import tpu as pltpu

F32 = jnp.float32
BF16 = jnp.bfloat16

D_MODEL = 1024
DEPTH = 4
N_PAIR = DEPTH // 2
HEAD_DIM = 64
ROT_DIM = HEAD_DIM // 4
ROPE_THETA = 500000.0
N_HEADS = 8
N_KV = 2
GROUP = N_HEADS // N_KV
QW = N_HEADS * HEAD_DIM
KVW = N_KV * HEAD_DIM
MOBA_BLOCK = 256
MOBA_TOPK = 3
Q_CHUNK = 128
PAGE_SIZE = 128
B_CH = 512
B_CONV_WIDTH = 31
WINDOW = 128
D_HEADS = 4
D_DK = 128
D_FF = 2816
FFN_CONV_WIDTH = 3
LN_EPS = 1e-5
ALPHA = (2.0 * DEPTH) ** 0.25
SCALE = HEAD_DIM ** -0.5
EVEN_IN = QW + 2 * KVW + 2 * B_CH
ODD_IN = QW + 2 * KVW + 4 * 512

LANES = 128
NEG = -1e30
VMEM_LIMIT = 56 * 1024 * 1024

HIGHEST = lax.Precision.HIGHEST


def _nn(a, b, precision=None):
    return lax.dot_general(a, b, (((1,), (0,)), ((), ())), preferred_element_type=F32, precision=precision)


def _nt(a, b, precision=None):
    return lax.dot_general(a, b, (((1,), (1,)), ((), ())), preferred_element_type=F32, precision=precision)


def _ln(z, g, b):
    mu = jnp.mean(z, axis=-1, keepdims=True)
    d = z - mu
    var = jnp.mean(d * d, axis=-1, keepdims=True)
    return d * lax.rsqrt(var + LN_EPS) * g + b


def _silu(x):
    return x * jax.nn.sigmoid(x)


def _params(sem):
    return pltpu.CompilerParams(dimension_semantics=sem, vmem_limit_bytes=VMEM_LIMIT)


def _layer_spec(w, layer):
    zeros = (0,) * (w.ndim - 1)
    return pl.BlockSpec((None,) + w.shape[1:], lambda *_: (layer,) + zeros)


def _rope_tables(pos):
    half = ROT_DIM // 2
    inv = ROPE_THETA ** (-jnp.arange(half, dtype=F32) * 2.0 / ROT_DIM)
    ang = pos.astype(F32)[:, None] * inv[None, :]
    cos, sin = jnp.cos(ang), jnp.sin(ang)
    p = pos.shape[0]
    one = jnp.ones((p, HEAD_DIM - ROT_DIM), F32)
    zero = jnp.zeros((p, HEAD_DIM - ROT_DIM), F32)
    zh = jnp.zeros((p, half), F32)
    c = jnp.concatenate([cos, cos, one], 1)
    sa = jnp.concatenate([-sin, zh, zero], 1)
    sb = jnp.concatenate([zh, sin, zero], 1)
    return tuple(jnp.tile(t, (1, LANES // HEAD_DIM)) for t in (c, sa, sb))


def _rope(z, c, sa, sb):
    return z * c + pltpu.roll(z, LANES - ROT_DIM // 2, 1) * sa + pltpu.roll(z, ROT_DIM // 2, 1) * sb


def _inproj_even_body(x_ref, w_ref, c_ref, sa_ref, sb_ref, q_ref, k_ref, v_ref, u_ref):
    xb = x_ref[...].astype(BF16)
    c, sa, sb = c_ref[...], sa_ref[...], sb_ref[...]
    hq = _nn(xb, w_ref[:, 0:QW])
    for s in range(QW // LANES):
        q_ref[:, LANES * s:LANES * (s + 1)] = _rope(hq[:, LANES * s:LANES * (s + 1)], c, sa, sb)
    hkv = _nn(xb, w_ref[:, QW:QW + 2 * KVW])
    k_ref[...] = _rope(hkv[:, 0:KVW], c, sa, sb)
    v_ref[...] = hkv[:, KVW:2 * KVW]
    o = QW + 2 * KVW
    ga = _nn(xb, w_ref[:, o:o + B_CH])
    gb = _nn(xb, w_ref[:, o + B_CH:o + 2 * B_CH])
    u_ref[...] = ga * jax.nn.sigmoid(gb)


def _inproj_odd_body(x_ref, w_ref, c_ref, sa_ref, sb_ref, q_ref, k_ref, v_ref, d_ref):
    xb = x_ref[...].astype(BF16)
    c, sa, sb = c_ref[...], sa_ref[...], sb_ref[...]
    hq = _nn(xb, w_ref[:, 0:QW])
    for s in range(QW // LANES):
        q_ref[:, LANES * s:LANES * (s + 1)] = _rope(hq[:, LANES * s:LANES * (s + 1)], c, sa, sb)
    hkv = _nn(xb, w_ref[:, QW:QW + 2 * KVW])
    k_ref[...] = _rope(hkv[:, 0:KVW], c, sa, sb)
    v_ref[...] = hkv[:, KVW:2 * KVW]
    o = QW + 2 * KVW
    d_ref[:, 0:512] = _silu(_nn(xb, w_ref[:, o:o + 512]))
    d_ref[:, 512:1536] = _nn(xb, w_ref[:, o + 512:o + 1536])
    d_ref[:, 1536:2048] = _silu(_nn(xb, w_ref[:, o + 1536:o + 2048]))


def _inproj(x2, w_bf, layer, tabs, even, tm):
    m = x2.shape[0]
    ntab = tabs[0].shape[0] // tm
    row = lambda i: (i, 0)
    tab = lambda i: (i % ntab, 0)
    wide = B_CH if even else 2048
    return pl.pallas_call(
        _inproj_even_body if even else _inproj_odd_body,
        grid=(m // tm,),
        in_specs=[pl.BlockSpec((tm, D_MODEL), row), _layer_spec(w_bf, layer),
                  pl.BlockSpec((tm, LANES), tab), pl.BlockSpec((tm, LANES), tab), pl.BlockSpec((tm, LANES), tab)],
        out_specs=[pl.BlockSpec((tm, QW), row), pl.BlockSpec((tm, KVW), row), pl.BlockSpec((tm, KVW), row),
                   pl.BlockSpec((tm, wide), row)],
        out_shape=[jax.ShapeDtypeStruct((m, QW), F32), jax.ShapeDtypeStruct((m, KVW), F32),
                   jax.ShapeDtypeStruct((m, KVW), F32), jax.ShapeDtypeStruct((m, wide), F32)],
        compiler_params=_params(("parallel",)),
        name="inproj_even" if even else "inproj_odd",
    )(x2, w_bf, *tabs)


def _queries_t(q):
    return jnp.concatenate([q[:, LANES * s:LANES * (s + 1)].T for s in range(QW // LANES)], 0)


def _stack_heads_t(qt, j):
    t = qt.shape[1]
    z = jnp.zeros((HEAD_DIM, t), F32)
    blocks = []
    for g in range(GROUP):
        h = GROUP * j + g
        hd = qt[HEAD_DIM * h:HEAD_DIM * (h + 1), :]
        blocks.append(jnp.concatenate([hd, z] if j == 0 else [z, hd], 0))
    return jnp.concatenate(blocks, 1)


def _unstack_heads(outs, t):
    lane = lax.broadcasted_iota(jnp.int32, (t, LANES), 1)
    slabs = []
    for p in range(N_HEADS // 2):
        halves = []
        for hh in range(2):
            h = 2 * p + hh
            j, g = h // GROUP, h % GROUP
            blk = outs[j][g * t:(g + 1) * t]
            if j != hh:
                blk = pltpu.roll(blk, HEAD_DIM, 1)
            halves.append(blk)
        slabs.append(jnp.where(lane < HEAD_DIM, halves[0], halves[1]))
    return jnp.concatenate(slabs, 1)


def _untranspose(o_t, t):
    return jnp.concatenate([o_t[:, t * g:t * (g + 1)].T for g in range(GROUP)], 0)


def _moba_prompt_body(q_ref, k_ref, v_ref, o_ref, km_ref, vt_ref, sel_ref, *, nb):
    c = pl.program_id(1)
    blk = c // (MOBA_BLOCK // Q_CHUNK)
    qs0 = c * Q_CHUNK
    cols = N_HEADS * Q_CHUNK
    nbp = km_ref.shape[0]

    @pl.when(c == 0)
    def _():
        km_ref[...] = jnp.concatenate(
            [jnp.mean(k_ref[n * MOBA_BLOCK:(n + 1) * MOBA_BLOCK, :], axis=0, keepdims=True) for n in range(nb)]
            + [jnp.zeros((nbp - nb, LANES), F32)] * (nbp > nb), 0)
        for n in range(nb):
            vt_ref[n] = v_ref[n * MOBA_BLOCK:(n + 1) * MOBA_BLOCK, :].T.astype(BF16)

    qt = _queries_t(q_ref[...] * SCALE)
    blk_row = lax.broadcasted_iota(jnp.int32, (nbp, cols), 0)
    kidx = lax.broadcasted_iota(jnp.int32, (MOBA_BLOCK, cols), 0)
    qpos = qs0 + lax.broadcasted_iota(jnp.int32, (MOBA_BLOCK, cols), 1) % Q_CHUNK
    qs_t = jnp.concatenate([_stack_heads_t(qt, j) for j in range(N_KV)], 1)
    qsb = qs_t.astype(BF16)
    gate = jnp.where(blk_row < blk, _nn(km_ref[...], qs_t, HIGHEST), -jnp.inf)
    sel_rows = []
    for n in range(nb):
        gn = gate[n:n + 1, :]
        beats = jnp.where((gate > gn) | ((gate == gn) & (blk_row < n)), 1.0, 0.0)
        cnt = jnp.sum(beats, axis=0, keepdims=True)
        sel_rows.append(jnp.where((cnt < MOBA_TOPK) & (n < blk), 1.0, 0.0))
    sel_ref[...] = jnp.concatenate(sel_rows + [jnp.zeros((nbp - nb, cols), F32)] * (nbp > nb), 0)

    def scores(n):
        off = pl.multiple_of(n * MOBA_BLOCK, MOBA_BLOCK)
        return _nn(k_ref[pl.ds(off, MOBA_BLOCK), :].astype(BF16), qsb), vt_ref[n]

    s, vt = scores(blk)
    s = jnp.where(kidx + blk * MOBA_BLOCK <= qpos, s, NEG)
    m = jnp.max(s, axis=0, keepdims=True)
    p = jnp.exp(s - m)
    l = jnp.sum(p, axis=0, keepdims=True)
    acc = _nn(vt, p.astype(BF16))

    def past(i, carry):
        m, l, acc = carry
        n0 = 2 * i
        n1 = n0 + 1
        s0, vt0 = scores(n0)
        s1, vt1 = scores(n1)
        s0 = jnp.where(sel_ref[pl.ds(n0, 1), :] > 0.5, s0, NEG)
        s1 = jnp.where(sel_ref[pl.ds(n1, 1), :] > 0.5, s1, NEG)
        m_new = jnp.maximum(m, jnp.maximum(jnp.max(s0, axis=0, keepdims=True), jnp.max(s1, axis=0, keepdims=True)))
        a = jnp.exp(m - m_new)
        p0 = jnp.exp(s0 - m_new)
        p1 = jnp.exp(s1 - m_new)
        l_new = a * l + jnp.sum(p0, axis=0, keepdims=True) + jnp.sum(p1, axis=0, keepdims=True)
        return m_new, l_new, a * acc + _nn(vt0, p0.astype(BF16)) + _nn(vt1, p1.astype(BF16))

    m, l, acc = lax.fori_loop(0, (blk + 1) // 2, past, (m, l, acc))
    o_t = acc / l
    half = GROUP * Q_CHUNK
    o = _unstack_heads([_untranspose(o_t[:, half * j:half * (j + 1)], Q_CHUNK) for j in range(N_KV)], Q_CHUNK)
    o_ref[...] = o.astype(o_ref.dtype)


def _moba_prompt(q, k, v):
    b, l, _ = q.shape
    nb = l // MOBA_BLOCK
    nbp = -(-nb // 8) * 8
    return pl.pallas_call(
        functools.partial(_moba_prompt_body, nb=nb),
        grid=(b, l // Q_CHUNK),
        in_specs=[pl.BlockSpec((None, Q_CHUNK, QW), lambda i, c: (i, c, 0)),
                  pl.BlockSpec((None, l, KVW), lambda i, c: (i, 0, 0)),
                  pl.BlockSpec((None, l, KVW), lambda i, c: (i, 0, 0))],
        out_specs=pl.BlockSpec((None, Q_CHUNK, QW), lambda i, c: (i, c, 0)),
        out_shape=jax.ShapeDtypeStruct((b, l, QW), BF16),
        scratch_shapes=[pltpu.VMEM((nbp, LANES), F32), pltpu.VMEM((nb, KVW, MOBA_BLOCK), BF16),
                        pltpu.VMEM((nbp, N_HEADS * Q_CHUNK), F32)],
        compiler_params=_params(("parallel", "arbitrary")),
        name="moba_prompt",
    )(q, k, v)


def _stack_decode(q):
    s = q.shape[0]
    lane = lax.broadcasted_iota(jnp.int32, (s, LANES), 1)
    parts = []
    for h in range(N_HEADS):
        j = h // GROUP
        slab = q[:, LANES * (h // 2):LANES * (h // 2 + 1)]
        if h % 2 != j:
            slab = pltpu.roll(slab, HEAD_DIM, 1)
        keep = (lane < HEAD_DIM) if j == 0 else (lane >= HEAD_DIM)
        parts.append(jnp.where(keep, slab, 0.0))
    return jnp.concatenate(parts, 0)


def _unstack_decode(o, s):
    lane = lax.broadcasted_iota(jnp.int32, (s, LANES), 1)
    slabs = []
    for p in range(N_HEADS // 2):
        halves = []
        for hh in range(2):
            h = 2 * p + hh
            blk = o[h * s:(h + 1) * s]
            if h // GROUP != hh:
                blk = pltpu.roll(blk, HEAD_DIM, 1)
            halves.append(blk)
        slabs.append(jnp.where(lane < HEAD_DIM, halves[0], halves[1]))
    return jnp.concatenate(slabs, 1)


def _pad_rows(x, n):
    return jnp.concatenate([x, jnp.zeros((n - x.shape[0], x.shape[1]), x.dtype)], 0)


def _kv_transposed(x):
    lead = x.shape[:-3]
    n = len(lead)
    t = x.shape[-3]
    return jnp.transpose(x, tuple(range(n)) + (n + 1, n + 2, n)).reshape(lead + (KVW, t))


def _kv_untransposed(xt):
    lead = xt.shape[:-2]
    n = len(lead)
    t = xt.shape[-1]
    return jnp.transpose(xt.reshape(lead + (N_KV, HEAD_DIM, t)), tuple(range(n)) + (n + 2, n, n + 1))


def _moba_sample_body(pt_ref, q_ref, kn_ref, vn_ref, kpool_ref, vpool_ref, o_ref, kbuf_ref, vbuf_ref, kb_ref, s_ref,
                      sem_ref, *, nb, s_len, layer, nbatch):
    ppb = MOBA_BLOCK // PAGE_SIZE
    assert ppb == 2 and 2 * N_HEADS * s_len == LANES
    npages = nb * ppb
    rows = N_HEADS * s_len
    nbp = -(-nb // 8) * 8
    b = pl.program_id(0)
    slot = b % 2
    nslot = 1 - slot
    nxt = jnp.minimum(b + 1, nbatch - 1)

    def k_copy(bi, p, sl):
        return pltpu.make_async_copy(kpool_ref.at[layer, pt_ref[bi, p]], kbuf_ref.at[sl, p], sem_ref.at[sl, 0])

    def v_copy(bi, p, sl):
        return pltpu.make_async_copy(vpool_ref.at[layer, pt_ref[bi, p]], vbuf_ref.at[sl, p], sem_ref.at[sl, 1])

    @pl.when(b == 0)
    def _():
        for p in range(npages):
            k_copy(0, p, 0).start()
            v_copy(0, p, 0).start()

    for p in range(npages):
        k_copy(b, p, slot).wait()
        v_copy(b, p, slot).wait()

    qa = _stack_decode(q_ref[...] * SCALE)
    qt = jnp.concatenate([qa, qa], 0).T
    lane = lax.broadcasted_iota(jnp.int32, (LANES, LANES), 1)
    lo = lane < rows
    qt_lo = jnp.where(lo, qt, 0.0).astype(BF16)
    qt_hi = jnp.where(lo, 0.0, qt).astype(BF16)
    sums = []
    for p in range(npages):
        k_copy(nxt, p, nslot).start()
        v_copy(nxt, p, nslot).start()
        kpg = kbuf_ref[slot, p].T
        kb_ref[p] = kpg.astype(BF16)
        sums.append(jnp.sum(kpg, axis=0, keepdims=True))
    km = jnp.concatenate([sum(sums[ppb * n:ppb * (n + 1)]) * (1.0 / MOBA_BLOCK) for n in range(nb)]
                         + [jnp.zeros((nbp - nb, LANES), F32)] * (nbp > nb), 0)
    blk_row = lax.broadcasted_iota(jnp.int32, (nbp, LANES), 0)
    gate = jnp.where(blk_row < nb, _nn(km, qt, HIGHEST), -jnp.inf)
    sel = []
    for n in range(nb):
        gn = gate[n:n + 1, :]
        beats = jnp.where((gate > gn) | ((gate == gn) & (blk_row < n)), 1.0, 0.0)
        sel.append(jnp.sum(beats, axis=0, keepdims=True) < MOBA_TOPK)
    kidx = lax.broadcasted_iota(jnp.int32, (LANES, LANES), 0)
    tok = lane % s_len
    s_new = jnp.where((kidx <= tok) & (kidx < s_len) & lo, _nn(_pad_rows(kn_ref[...], LANES).astype(BF16), qt_lo), NEG)
    m = jnp.max(s_new, axis=0, keepdims=True)
    for n in range(nb):
        s = jnp.where(sel[n], _nn(kb_ref[ppb * n], qt_lo) + _nn(kb_ref[ppb * n + 1], qt_hi), NEG)
        s_ref[n] = s
        m = jnp.maximum(m, jnp.max(s, axis=0, keepdims=True))
    m = jnp.maximum(m, pltpu.roll(m, rows, 1))
    p_new = jnp.exp(s_new - m)
    l = jnp.sum(p_new, axis=0, keepdims=True)
    acc_lo = _nn(_pad_rows(vn_ref[...], LANES).T.astype(BF16), p_new.astype(BF16))
    acc_hi = jnp.zeros((KVW, LANES), F32)
    for n in range(nb):
        pp = jnp.exp(s_ref[n] - m)
        l = l + jnp.sum(pp, axis=0, keepdims=True)
        ppb16 = pp.astype(BF16)
        acc_lo = acc_lo + _nn(vbuf_ref[slot, ppb * n].astype(BF16), ppb16)
        acc_hi = acc_hi + _nn(vbuf_ref[slot, ppb * n + 1].astype(BF16), ppb16)
    acc = jnp.where(lo, acc_lo, acc_hi)
    o_t = (acc + pltpu.roll(acc, rows, 1)) / (l + pltpu.roll(l, rows, 1))
    o_ref[...] = _unstack_decode(o_t.T[0:rows], s_len)

    @pl.when(b == nbatch - 1)
    def _():
        for p in range(npages):
            k_copy(nxt, p, nslot).wait()
            v_copy(nxt, p, nslot).wait()


def _moba_sample(q, k_new, v_new, k_pool_t, v_pool_t, page_table, layer):
    b, s_len, _ = q.shape
    npages = page_table.shape[1]
    assert (npages * PAGE_SIZE) % MOBA_BLOCK == 0 and N_HEADS * s_len <= LANES
    nb = npages * PAGE_SIZE // MOBA_BLOCK
    assert 0 < nb <= LANES

    tok = lambda i, pt: (i, 0, 0)
    page_buf = pltpu.VMEM((2, npages, KVW, PAGE_SIZE), F32)
    grid_spec = pltpu.PrefetchScalarGridSpec(
        num_scalar_prefetch=1,
        grid=(b,),
        in_specs=[pl.BlockSpec((None, s_len, QW), tok), pl.BlockSpec((None, s_len, KVW), tok),
                  pl.BlockSpec((None, s_len, KVW), tok),
                  pl.BlockSpec(memory_space=pl.ANY), pl.BlockSpec(memory_space=pl.ANY)],
        out_specs=pl.BlockSpec((None, s_len, QW), tok),
        scratch_shapes=[page_buf, page_buf, pltpu.VMEM((npages, PAGE_SIZE, KVW), BF16),
                        pltpu.VMEM((nb, PAGE_SIZE, LANES), F32), pltpu.SemaphoreType.DMA((2, 2))],
    )
    return pl.pallas_call(
        functools.partial(_moba_sample_body, nb=nb, s_len=s_len, layer=layer, nbatch=b),
        grid_spec=grid_spec,
        out_shape=jax.ShapeDtypeStruct((b, s_len, QW), F32),
        compiler_params=_params(("arbitrary",)),
        name="moba_sample",
    )(page_table, q, k_new, v_new, k_pool_t, v_pool_t)


def _swa_prompt_body(q_ref, kp_ref, kc_ref, vp_ref, vc_ref, sink_ref, o_ref, *, nw):
    t = pl.program_id(1)
    cols = N_HEADS * WINDOW
    kidx = lax.broadcasted_iota(jnp.int32, (2 * WINDOW, cols), 0)
    tq = lax.broadcasted_iota(jnp.int32, (2 * WINDOW, cols), 1) % WINDOW
    band = (kidx >= tq) & (kidx <= tq + WINDOW)
    sink = sink_ref[...]
    half = GROUP * WINDOW
    kblk = [kp_ref[...]] + [kc_ref[WINDOW * w:WINDOW * (w + 1), :] for w in range(nw)]
    vblk_t = [vp_ref[...].T] + [vc_ref[WINDOW * w:WINDOW * (w + 1), :].T for w in range(nw)]
    for w in range(nw):
        qt = _queries_t(q_ref[WINDOW * w:WINDOW * (w + 1), :] * SCALE)
        kk = jnp.concatenate(kblk[w:w + 2], 0).astype(BF16)
        vv_t = jnp.concatenate(vblk_t[w:w + 2], 1).astype(BF16)
        valid = band & (kidx >= jnp.where(t == 0, WINDOW, 0)) if w == 0 else band
        qs_t = jnp.concatenate([_stack_heads_t(qt, j) for j in range(N_KV)], 1)
        s = jnp.where(valid, _nn(kk, qs_t.astype(BF16)), NEG)
        m = jnp.maximum(jnp.max(s, axis=0, keepdims=True), sink)
        p = jnp.exp(s - m)
        den = jnp.sum(p, axis=0, keepdims=True) + jnp.exp(sink - m)
        o_t = _nn(vv_t, p.astype(BF16)) / den
        o = _unstack_heads([_untranspose(o_t[:, half * j:half * (j + 1)], WINDOW) for j in range(N_KV)], WINDOW)
        o_ref[WINDOW * w:WINDOW * (w + 1), :] = o.astype(o_ref.dtype)


def _swa_prompt(q, k, v, sink, nw=4):
    b, l, _ = q.shape
    assert l % (nw * WINDOW) == 0
    sink_cols = jnp.repeat(sink.astype(F32), WINDOW)[None, :]
    cur = lambda i, t: (i, t, 0)
    prev = lambda i, t: (i, jnp.maximum(nw * t - 1, 0), 0)
    return pl.pallas_call(
        functools.partial(_swa_prompt_body, nw=nw),
        grid=(b, l // (nw * WINDOW)),
        in_specs=[pl.BlockSpec((None, nw * WINDOW, QW), cur),
                  pl.BlockSpec((None, WINDOW, KVW), prev), pl.BlockSpec((None, nw * WINDOW, KVW), cur),
                  pl.BlockSpec((None, WINDOW, KVW), prev), pl.BlockSpec((None, nw * WINDOW, KVW), cur),
                  pl.BlockSpec((1, N_HEADS * WINDOW), lambda i, t: (0, 0))],
        out_specs=pl.BlockSpec((None, nw * WINDOW, QW), cur),
        out_shape=jax.ShapeDtypeStruct((b, l, QW), BF16),
        compiler_params=_params(("parallel", "arbitrary")),
        name="swa_prompt",
    )(q, k, k, v, v, sink_cols)


def _swa_sample_body(q_ref, kn_ref, vn_ref, wk_ref, wv_ref, sink_ref, o_ref, nk_ref, nv_ref, *, bt, s_len):
    rows = N_HEADS * s_len
    tok = lax.broadcasted_iota(jnp.int32, (rows, LANES), 0) % s_len
    lane = lax.broadcasted_iota(jnp.int32, (rows, LANES), 1)
    lane_w = lax.broadcasted_iota(jnp.int32, (KVW, WINDOW), 1)
    sink = sink_ref[...]

    def one(i, carry):
        qab = _stack_decode(q_ref[i] * SCALE).astype(BF16)
        kn = _pad_rows(kn_ref[i], LANES)
        vn = _pad_rows(vn_ref[i], LANES)
        wk_t = wk_ref[i]
        wv_t = wv_ref[i]
        s_win = jnp.where(lane >= tok, _nn(qab, wk_t.astype(BF16)), NEG)
        s_new = jnp.where((lane <= tok) & (lane < s_len), _nt(qab, kn.astype(BF16)), NEG)
        m = jnp.maximum(jnp.maximum(jnp.max(s_win, axis=1, keepdims=True), jnp.max(s_new, axis=1, keepdims=True)), sink)
        p_win = jnp.exp(s_win - m)
        p_new = jnp.exp(s_new - m)
        den = jnp.sum(p_win, axis=1, keepdims=True) + jnp.sum(p_new, axis=1, keepdims=True) + jnp.exp(sink - m)
        acc = _nt(p_win.astype(BF16), wv_t.astype(BF16)) + _nn(p_new.astype(BF16), vn.astype(BF16))
        o_ref[i] = _unstack_decode(acc / den, s_len)
        keep = lane_w < WINDOW - s_len
        nk_ref[i] = jnp.where(keep, pltpu.roll(wk_t, WINDOW - s_len, 1), pltpu.roll(kn.T, WINDOW - s_len, 1))
        nv_ref[i] = jnp.where(keep, pltpu.roll(wv_t, WINDOW - s_len, 1), pltpu.roll(vn.T, WINDOW - s_len, 1))
        return carry

    def pair(i, carry):
        return one(2 * i + 1, one(2 * i, carry))

    assert bt % 2 == 0
    lax.fori_loop(0, bt // 2, pair, 0)


def _swa_sample(q, k_new, v_new, win_k_t, win_v_t, layer, sink, bt=16):
    b, s_len, _ = q.shape
    bt = min(bt, b)
    assert win_k_t.shape[3] == WINDOW == LANES and b % bt == 0 and s_len <= WINDOW
    sink_rows = jnp.repeat(sink.astype(F32), s_len)[:, None]
    blk = lambda i: (i, 0, 0)
    win = pl.BlockSpec((bt, KVW, WINDOW), blk)
    win_in = pl.BlockSpec((None, bt, KVW, WINDOW), lambda i: (layer, i, 0, 0))
    return pl.pallas_call(
        functools.partial(_swa_sample_body, bt=bt, s_len=s_len),
        grid=(b // bt,),
        in_specs=[pl.BlockSpec((bt, s_len, QW), blk), pl.BlockSpec((bt, s_len, KVW), blk),
                  pl.BlockSpec((bt, s_len, KVW), blk), win_in, win_in,
                  pl.BlockSpec((N_HEADS * s_len, 1), lambda i: (0, 0))],
        out_specs=[pl.BlockSpec((bt, s_len, QW), blk), win, win],
        out_shape=[jax.ShapeDtypeStruct((b, s_len, QW), F32), jax.ShapeDtypeStruct((b, KVW, WINDOW), F32),
                   jax.ShapeDtypeStruct((b, KVW, WINDOW), F32)],
        compiler_params=_params(("parallel",)),
        name="swa_sample",
    )(q, k_new, v_new, win_k_t, win_v_t, sink_rows)


def _conv_prompt_body(prev_ref, cur_ref, w_ref, b_ref, g_ref, bb_ref, o_ref, ext_ref, sh_ref, raw_ref, *, tm, halo):
    t = pl.program_id(1)
    ext_ref[0:halo, :] = jnp.where(t == 0, 0.0, prev_ref[...])
    ext_ref[halo:halo + tm, :] = cur_ref[...]
    kw = w_ref.shape[0]
    base = halo - (kw - 1)
    sub = 8
    for lc in range(B_CH // LANES):
        ln = slice(LANES * lc, LANES * (lc + 1))
        acc = jnp.zeros((tm, LANES), F32) + b_ref[:, ln]
        for r in range(sub):
            offs = [o for o in range(base, base + kw) if o % sub == r]
            if not offs:
                continue
            n = tm + max(offs) - r
            if r:
                sh_ref[0:n, :] = ext_ref[pl.ds(r, n), ln]
            src = sh_ref if r else ext_ref.at[:, ln]
            for o in offs:
                acc = acc + w_ref[o - base:o - base + 1, ln] * src[pl.ds(o - r, tm), :]
        raw_ref[:, ln] = acc
    o_ref[...] = _silu(_ln(raw_ref[...], g_ref[...], bb_ref[...])).astype(o_ref.dtype)


def _conv_prompt(u, w, b, g, bb, tm=256, halo=32):
    bsz, l, _ = u.shape
    r = tm // halo
    vec = lambda i, t: (0, 0)
    return pl.pallas_call(
        functools.partial(_conv_prompt_body, tm=tm, halo=halo),
        grid=(bsz, l // tm),
        in_specs=[pl.BlockSpec((None, halo, B_CH), lambda i, t: (i, jnp.maximum(t * r - 1, 0), 0)),
                  pl.BlockSpec((None, tm, B_CH), lambda i, t: (i, t, 0)),
                  pl.BlockSpec(w.shape, vec), pl.BlockSpec((1, B_CH), vec), pl.BlockSpec((1, B_CH), vec),
                  pl.BlockSpec((1, B_CH), vec)],
        out_specs=pl.BlockSpec((None, tm, B_CH), lambda i, t: (i, t, 0)),
        out_shape=jax.ShapeDtypeStruct((bsz, l, B_CH), BF16),
        scratch_shapes=[pltpu.VMEM((tm + halo, B_CH), F32), pltpu.VMEM((tm + halo, LANES), F32),
                        pltpu.VMEM((tm, B_CH), F32)],
        compiler_params=_params(("parallel", "arbitrary")),
        name="conv_prompt",
    )(u, u, w, b[None], g[None], bb[None])


def _conv_sample_body(up_ref, w_ref, b_ref, g_ref, bb_ref, o_ref, *, bt, s_len):
    kw = w_ref.shape[0]
    acc = jnp.zeros((bt, s_len, B_CH), F32) + b_ref[...]
    for j in range(kw):
        acc = acc + w_ref[j:j + 1, :] * up_ref[:, pl.ds(j, s_len), :]
    o_ref[...] = _silu(_ln(acc, g_ref[...], bb_ref[...]))


def _conv_sample(up, w, b, g, bb, bt=32):
    bsz, rows, _ = up.shape
    bt = min(bt, bsz)
    s_len = rows - (w.shape[0] - 1)
    vec = lambda i: (0, 0)
    return pl.pallas_call(
        functools.partial(_conv_sample_body, bt=bt, s_len=s_len),
        grid=(bsz // bt,),
        in_specs=[pl.BlockSpec((bt, rows, B_CH), lambda i: (i, 0, 0)),
                  pl.BlockSpec(w.shape, vec), pl.BlockSpec((1, B_CH), vec), pl.BlockSpec((1, B_CH), vec),
                  pl.BlockSpec((1, B_CH), vec)],
        out_specs=pl.BlockSpec((bt, s_len, B_CH), lambda i: (i, 0, 0)),
        out_shape=jax.ShapeDtypeStruct((bsz, s_len, B_CH), F32),
        compiler_params=_params(("parallel",)),
        name="conv_sample",
    )(up, w, b[None], g[None], bb[None])


def _hgrn_tile(hq, df, v, lb, ch):
    n = LANES // ch
    loglb, log1mlb, onemlb = lb[0:1, :], lb[1:2, :], lb[2:3, :]
    e = jnp.exp(-jnp.abs(df))
    ls = jnp.minimum(df, 0.0) - jnp.log1p(e)
    b_ = log1mlb + ls
    mx = jnp.maximum(loglb, b_)
    logf = mx + jnp.log(jnp.exp(loglb - mx) + jnp.exp(b_ - mx))
    hk = onemlb * (jnp.where(df >= 0.0, e, 1.0) / (1.0 + e))
    row = lax.broadcasted_iota(jnp.int32, (LANES, LANES), 0)
    colv = lax.broadcasted_iota(jnp.int32, (LANES, LANES), 1)
    r = row % ch
    g = logf
    sh = 1
    while sh < ch:
        g = g + jnp.where(r >= sh, pltpu.roll(g, sh, 0), 0.0)
        sh *= 2
    qa, kb, kd, qe, dec = [], [], [], [], []
    for c in range(n):
        sl = slice(ch * c, ch * (c + 1))
        gc = g[sl]
        gm = gc[ch // 2 - 1:ch // 2]
        gl = gc[ch - 1:ch]
        qa.append(hq[sl] * jnp.exp(gc - gm))
        kb.append(hk[sl] * jnp.exp(gm - gc))
        kd.append(hk[sl] * jnp.exp(gl - gc))
        qe.append((hq[sl] * jnp.exp(gc)).astype(BF16))
        dec.append(jnp.exp(gl))
    att = _nt(jnp.concatenate(qa, 0).astype(BF16), jnp.concatenate(kb, 0).astype(BF16))
    att = jnp.where((row // ch == colv // ch) & (colv <= row), att, 0.0)
    vb = v.astype(BF16)
    o_intra = _nn(att.astype(BF16), vb)
    vt = v.T
    kdb = jnp.concatenate(kd, 0).astype(BF16)
    ut = [_nn(jnp.where(colv // ch == c, vt, 0.0).astype(BF16), kdb) for c in range(n)]
    return o_intra, ut, qe, dec


def _hgrn_finish(o, sg, gn):
    return o * lax.rsqrt(jnp.mean(o * o, axis=-1, keepdims=True) + LN_EPS) * gn * sg


def _hgrn_prompt_body(hq_ref, df_ref, di_ref, sg_ref, lb_ref, gn_ref, o_ref, s_ref, *, ngroups, ch, nh):
    def group(gi, sts):
        off = pl.multiple_of(gi * LANES, LANES)
        rows = pl.ds(off, LANES)
        new = []
        for hh in range(nh):
            ln = slice(LANES * hh, LANES * (hh + 1))
            st = sts[hh]
            o_intra, ut, qe, dec = _hgrn_tile(hq_ref[rows, ln], df_ref[rows, ln], di_ref[rows, ln], lb_ref[:, ln], ch)
            o_inter = []
            for c in range(LANES // ch):
                o_inter.append(_nt(qe[c], st.astype(BF16)))
                st = dec[c] * st + ut[c]
            o = o_intra + jnp.concatenate(o_inter, 0)
            o_ref[rows, ln] = _hgrn_finish(o, sg_ref[rows, ln], gn_ref[...]).astype(o_ref.dtype)
            new.append(st)
        return tuple(new)

    zero = jnp.zeros((LANES, LANES), F32)
    sts = lax.fori_loop(0, ngroups, group, (zero,) * nh)
    for hh in range(nh):
        s_ref[hh] = sts[hh].T


def _hgrn_prompt(d4, lbt, gn, ch=16, nh=4):
    b, l, _ = d4.shape
    hp = D_HEADS // nh
    seg = lambda s: pl.BlockSpec((None, l, nh * LANES), lambda i, h: (i, 0, hp * s + h))
    return pl.pallas_call(
        functools.partial(_hgrn_prompt_body, ngroups=l // LANES, ch=ch, nh=nh),
        grid=(b, hp),
        in_specs=[seg(0), seg(1), seg(2), seg(3),
                  pl.BlockSpec((3, nh * LANES), lambda i, h: (0, h)), pl.BlockSpec((1, LANES), lambda i, h: (0, 0))],
        out_specs=[pl.BlockSpec((None, l, nh * LANES), lambda i, h: (i, 0, h)),
                   pl.BlockSpec((None, nh, D_DK, LANES), lambda i, h: (i, h, 0, 0))],
        out_shape=[jax.ShapeDtypeStruct((b, l, D_HEADS * LANES), BF16),
                   jax.ShapeDtypeStruct((b, D_HEADS, D_DK, LANES), F32)],
        compiler_params=_params(("parallel", "arbitrary")),
        name="hgrn_prompt",
    )(d4, d4, d4, d4, lbt, gn[None])


def _hgrn_sample_body(hq_ref, df_ref, di_ref, sg_ref, lb_ref, gn_ref, s0_ref, o_ref, s_ref, *, ch):
    o_intra, ut, qe, dec = _hgrn_tile(hq_ref[...], df_ref[...], di_ref[...], lb_ref[...], ch)
    o_inter = []
    for c in range(LANES // ch):
        st = s0_ref[c].T
        o_inter.append(_nt(qe[c], st.astype(BF16)))
        s_ref[c] = (dec[c] * st + ut[c]).T
    o = o_intra + jnp.concatenate(o_inter, 0)
    o_ref[...] = _hgrn_finish(o, sg_ref[...], gn_ref[...])


def _hgrn_sample(d4, lbt, gn, s0, layer):
    m = d4.shape[0]
    b = s0.shape[1]
    s_len = m // b
    assert LANES % s_len == 0 and s_len & (s_len - 1) == 0 and m % LANES == 0
    bt = LANES // s_len
    seg = lambda s: pl.BlockSpec((LANES, LANES), lambda i, h: (i, D_HEADS * s + h))
    st_in = pl.BlockSpec((None, bt, None, D_DK, LANES), lambda i, h: (layer, i, h, 0, 0))
    st_spec = pl.BlockSpec((bt, None, D_DK, LANES), lambda i, h: (i, h, 0, 0))
    return pl.pallas_call(
        functools.partial(_hgrn_sample_body, ch=s_len),
        grid=(m // LANES, D_HEADS),
        in_specs=[seg(0), seg(1), seg(2), seg(3),
                  pl.BlockSpec((3, LANES), lambda i, h: (0, h)), pl.BlockSpec((1, LANES), lambda i, h: (0, 0)), st_in],
        out_specs=[pl.BlockSpec((LANES, LANES), lambda i, h: (i, h)), st_spec],
        out_shape=[jax.ShapeDtypeStruct((m, D_HEADS * LANES), F32), jax.ShapeDtypeStruct(s0.shape[1:], F32)],
        compiler_params=_params(("parallel", "arbitrary")),
        name="hgrn_sample",
    )(d4, d4, d4, d4, lbt, gn[None], s0)


def _outproj_body(a_ref, c_ref, x_ref, w_ref, g_ref, b_ref, o_ref):
    wa = a_ref.shape[1]
    y = _nn(a_ref[...].astype(BF16), w_ref[0:wa, :]) + _nn(c_ref[...].astype(BF16), w_ref[wa:, :])
    o_ref[...] = _ln(ALPHA * x_ref[...] + y, g_ref[...], b_ref[...])


def _outproj_ln(a, c, x2, w_bf, layer, g, b, tm):
    m = x2.shape[0]
    row = lambda i: (i, 0)
    vec = lambda i: (0, 0)
    return pl.pallas_call(
        _outproj_body,
        grid=(m // tm,),
        in_specs=[pl.BlockSpec((tm, a.shape[1]), row), pl.BlockSpec((tm, c.shape[1]), row),
                  pl.BlockSpec((tm, D_MODEL), row), _layer_spec(w_bf, layer),
                  pl.BlockSpec((1, D_MODEL), vec), pl.BlockSpec((1, D_MODEL), vec)],
        out_specs=pl.BlockSpec((tm, D_MODEL), row),
        out_shape=jax.ShapeDtypeStruct((m, D_MODEL), F32),
        compiler_params=_params(("parallel",)),
        name="outproj_ln",
    )(a, c, x2, w_bf, g[None], b[None])


def _ffn_body(x_ref, hist_ref, wg_ref, wu_ref, wd_ref, wc_ref, bc_ref, g_ref, b_ref,
              o_ref, st_ref, *carry, tm, seg, fc):
    x = x_ref[...]
    xb = x.astype(BF16)
    row = lax.broadcasted_iota(jnp.int32, (tm, fc), 0)
    r = row % seg
    if seg == tm:
        carry_ref, = carry

        @pl.when(pl.program_id(1) == 0)
        def _():
            carry_ref[...] = hist_ref[...]

    acc = jnp.zeros((tm, D_MODEL), F32)
    for c in range(D_FF // fc):
        cols = slice(fc * c, fc * (c + 1))
        gp = _nn(xb, wg_ref[:, cols])
        if seg == tm:
            h0 = carry_ref[0:1, cols]
            h1 = carry_ref[1:2, cols]
            p1 = jnp.where(row == 0, h1, pltpu.roll(gp, 1, 0))
            p2 = jnp.where(row == 0, h0, jnp.where(row == 1, h1, pltpu.roll(gp, 2, 0)))
            carry_ref[:, cols] = gp[tm - 2:tm]
            st_ref[:, cols] = gp[tm - 2:tm]
        else:
            hp = hist_ref[:, cols]
            p1 = jnp.where(r == 0, pltpu.roll(hp, (tm - (seg - 1)) % tm, 0), pltpu.roll(gp, 1, 0))
            p2 = jnp.where(r < 2, pltpu.roll(hp, (tm - (seg - 2)) % tm, 0), pltpu.roll(gp, 2, 0))
            st_ref[:, cols] = gp
        cv = wc_ref[0:1, cols] * p2 + wc_ref[1:2, cols] * p1 + wc_ref[2:3, cols] * gp + bc_ref[:, cols]
        hid = 0.5 * cv * (1.0 + lax.erf(cv * (2.0 ** -0.5))) * _nn(xb, wu_ref[:, cols])
        acc = acc + _nn(hid.astype(BF16), wd_ref[cols, :])
    o_ref[...] = _ln(ALPHA * x + acc, g_ref[...], b_ref[...])


def _ffn_prompt(x, hist, wg, wu, wd, layer, wc, bc, g, b, tm=1024, fc=256):
    bsz, l, _ = x.shape
    tm = min(tm, l)
    assert l % tm == 0
    vec = lambda i, t: (0, 0)
    return pl.pallas_call(
        functools.partial(_ffn_body, tm=tm, seg=tm, fc=fc),
        grid=(bsz, l // tm),
        in_specs=[pl.BlockSpec((None, tm, D_MODEL), lambda i, t: (i, t, 0)),
                  pl.BlockSpec((None, 2, D_FF), lambda i, t: (i, 0, 0)),
                  _layer_spec(wg, layer), _layer_spec(wu, layer), _layer_spec(wd, layer),
                  pl.BlockSpec(wc.shape, vec), pl.BlockSpec((1, D_FF), vec),
                  pl.BlockSpec((1, D_MODEL), vec), pl.BlockSpec((1, D_MODEL), vec)],
        out_specs=[pl.BlockSpec((None, tm, D_MODEL), lambda i, t: (i, t, 0)),
                   pl.BlockSpec((None, 2, D_FF), lambda i, t: (i, 0, 0))],
        out_shape=[jax.ShapeDtypeStruct((bsz, l, D_MODEL), F32), jax.ShapeDtypeStruct((bsz, 2, D_FF), F32)],
        scratch_shapes=[pltpu.VMEM((FFN_CONV_WIDTH - 1, D_FF), F32)],
        compiler_params=_params(("parallel", "arbitrary")),
        name="ffn_prompt",
    )(x, hist, wg, wu, wd, wc, bc[None], g[None], b[None])


def _ffn_sample(x2, hist_rows, s_len, wg, wu, wd, layer, wc, bc, g, b, tm=512, fc=256):
    m = x2.shape[0]
    tm = min(tm, m)
    vec = lambda i: (0, 0)
    row = lambda i: (i, 0)
    return pl.pallas_call(
        functools.partial(_ffn_body, tm=tm, seg=s_len, fc=fc),
        grid=(m // tm,),
        in_specs=[pl.BlockSpec((tm, D_MODEL), row), pl.BlockSpec((tm, D_FF), row),
                  _layer_spec(wg, layer), _layer_spec(wu, layer), _layer_spec(wd, layer),
                  pl.BlockSpec(wc.shape, vec), pl.BlockSpec((1, D_FF), vec),
                  pl.BlockSpec((1, D_MODEL), vec), pl.BlockSpec((1, D_MODEL), vec)],
        out_specs=[pl.BlockSpec((tm, D_MODEL), row), pl.BlockSpec((tm, D_FF), row)],
        out_shape=[jax.ShapeDtypeStruct((m, D_MODEL), F32), jax.ShapeDtypeStruct((m, D_FF), F32)],
        compiler_params=_params(("parallel",)),
        name="ffn_sample",
    )(x2, hist_rows, wg, wu, wd, wc, bc[None], g[None], b[None])


def _lb_tables(lb_param):
    lbs = jnp.cumsum(jax.nn.softmax(lb_param.astype(F32), axis=0), axis=0)
    lbs = lbs - lbs[:1]
    return jnp.stack([jnp.log(lbs), jnp.log1p(-lbs), 1.0 - lbs], axis=1)


def _trunk_prompt(x, prm, lbt):
    b, l, _ = x.shape
    m = b * l
    tm = 512
    tabs = _rope_tables(jnp.arange(l))
    kv_k, kv_v, conv_b, win_k, win_v, hgrn, ffn = [], [], [], [], [], [], []
    zero_hist = jnp.zeros((b, FFN_CONV_WIDTH - 1, D_FF), F32)
    for layer in range(DEPTH):
        i = layer // 2
        x2 = x.reshape(m, D_MODEL)
        if layer % 2 == 0:
            q, k, v, u = _inproj(x2, prm["w_in_even"], i, tabs, True, tm)
            k3, v3, u3 = k.reshape(b, l, KVW), v.reshape(b, l, KVW), u.reshape(b, l, B_CH)
            a = _moba_prompt(q.reshape(b, l, QW), k3, v3)
            cv = _conv_prompt(u3, prm["w_dw_b"][i], prm["b_dw_b"][i], prm["conv_ln_g"][i], prm["conv_ln_b"][i])
            kv_k.append(k3.reshape(b, l, N_KV, HEAD_DIM))
            kv_v.append(v3.reshape(b, l, N_KV, HEAD_DIM))
            conv_b.append(u3[:, l - (B_CONV_WIDTH - 1):])
            x1 = _outproj_ln(a.reshape(m, QW), cv.reshape(m, B_CH), x2, prm["w_out_even"], i,
                             prm["ln1_g"][layer], prm["ln1_b"][layer], tm)
        else:
            q, k, v, d4 = _inproj(x2, prm["w_in_odd"], i, tabs, False, tm)
            k3, v3 = k.reshape(b, l, KVW), v.reshape(b, l, KVW)
            a = _swa_prompt(q.reshape(b, l, QW), k3, v3, prm["sinks"][i])
            o, st = _hgrn_prompt(d4.reshape(b, l, 2048), lbt[i], prm["gnorm_g"][i])
            keep = min(WINDOW, l)
            win_k.append(k3[:, l - keep:].reshape(b, keep, N_KV, HEAD_DIM))
            win_v.append(v3[:, l - keep:].reshape(b, keep, N_KV, HEAD_DIM))
            hgrn.append(st)
            x1 = _outproj_ln(a.reshape(m, QW), o.reshape(m, 512), x2, prm["w_out_odd"], i,
                             prm["ln1_g"][layer], prm["ln1_b"][layer], tm)
        xo, fb = _ffn_prompt(x1.reshape(b, l, D_MODEL), zero_hist, prm["w_ffn_gate"], prm["w_ffn_up"],
                             prm["w_ffn_down"], layer, prm["w_ffn_conv"][layer], prm["b_ffn_conv"][layer],
                             prm["ln2_g"][layer], prm["ln2_b"][layer])
        ffn.append(fb)
        x = xo
    return (x, jnp.stack(kv_k), jnp.stack(kv_v), jnp.stack(conv_b), jnp.stack(win_k), jnp.stack(win_v),
            jnp.stack(hgrn), jnp.stack(ffn))


def _trunk_sample(x, prm, lbt, cache_k, cache_v, page_table, state_conv_b, state_win_k, state_win_v,
                  state_hgrn, state_ffn):
    b, s, _ = x.shape
    m = b * s
    tm = min(512, m)
    past = page_table.shape[1] * PAGE_SIZE
    assert state_win_k.shape[2] == WINDOW and past + s >= WINDOW
    tabs = _rope_tables(jnp.tile(past + jnp.arange(s), tm // s))
    pool_k_t = _kv_transposed(cache_k)
    pool_v_t = _kv_transposed(cache_v)
    win_k_t = _kv_transposed(state_win_k)
    win_v_t = _kv_transposed(state_win_v)
    kv_k, kv_v, conv_b, win_k, win_v, hgrn, ffn = [], [], [], [], [], [], []
    for layer in range(DEPTH):
        i = layer // 2
        x2 = x.reshape(m, D_MODEL)
        if layer % 2 == 0:
            q, k, v, u = _inproj(x2, prm["w_in_even"], i, tabs, True, tm)
            k3, v3 = k.reshape(b, s, KVW), v.reshape(b, s, KVW)
            a = _moba_sample(q.reshape(b, s, QW), k3, v3, pool_k_t, pool_v_t, page_table, i)
            up = jnp.concatenate([state_conv_b[i], u.reshape(b, s, B_CH)], 1)
            cv = _conv_sample(up, prm["w_dw_b"][i], prm["b_dw_b"][i], prm["conv_ln_g"][i], prm["conv_ln_b"][i])
            kv_k.append(k3.reshape(b, s, N_KV, HEAD_DIM))
            kv_v.append(v3.reshape(b, s, N_KV, HEAD_DIM))
            conv_b.append(up[:, up.shape[1] - (B_CONV_WIDTH - 1):])
            x1 = _outproj_ln(a.reshape(m, QW), cv.reshape(m, B_CH), x2, prm["w_out_even"], i,
                             prm["ln1_g"][layer], prm["ln1_b"][layer], tm)
        else:
            q, k, v, d4 = _inproj(x2, prm["w_in_odd"], i, tabs, False, tm)
            k3, v3 = k.reshape(b, s, KVW), v.reshape(b, s, KVW)
            a, nk_t, nv_t = _swa_sample(q.reshape(b, s, QW), k3, v3, win_k_t, win_v_t, i, prm["sinks"][i])
            o, st = _hgrn_sample(d4, lbt[i], prm["gnorm_g"][i], state_hgrn, i)
            win_k.append(_kv_untransposed(nk_t))
            win_v.append(_kv_untransposed(nv_t))
            hgrn.append(st)
            x1 = _outproj_ln(a.reshape(m, QW), o, x2, prm["w_out_odd"], i,
                             prm["ln1_g"][layer], prm["ln1_b"][layer], tm)
        hist_rows = jnp.pad(state_ffn[layer], ((0, 0), (s - (FFN_CONV_WIDTH - 1), 0), (0, 0))).reshape(m, D_FF)
        xo, gp = _ffn_sample(x1, hist_rows, s, prm["w_ffn_gate"], prm["w_ffn_up"],
                             prm["w_ffn_down"], layer, prm["w_ffn_conv"][layer], prm["b_ffn_conv"][layer],
                             prm["ln2_g"][layer], prm["ln2_b"][layer], tm=tm)
        ffn.append(gp.reshape(b, s, D_FF)[:, s - (FFN_CONV_WIDTH - 1):])
        x = xo.reshape(b, s, D_MODEL)
    return (x, jnp.stack(kv_k), jnp.stack(kv_v), jnp.stack(conv_b), jnp.stack(win_k), jnp.stack(win_v),
            jnp.stack(hgrn), jnp.stack(ffn))


def kernel(x_prompt, x_sample, cache_k, cache_v, state_conv_b, state_win_k, state_win_v, state_hgrn, state_ffn,
           page_table, w_in_even, w_dw_b, b_dw_b, conv_ln_g, conv_ln_b, w_out_even, w_in_odd, sinks, lb_param,
           gnorm_g, w_out_odd, ln1_g, ln1_b, ln2_g, ln2_b, w_ffn_gate, w_ffn_up, w_ffn_conv, b_ffn_conv, w_ffn_down):
    prm = dict(w_in_even=w_in_even.astype(BF16), w_dw_b=w_dw_b, b_dw_b=b_dw_b, conv_ln_g=conv_ln_g,
               conv_ln_b=conv_ln_b, w_out_even=w_out_even.astype(BF16), w_in_odd=w_in_odd.astype(BF16), sinks=sinks,
               gnorm_g=gnorm_g, w_out_odd=w_out_odd.astype(BF16), ln1_g=ln1_g, ln1_b=ln1_b, ln2_g=ln2_g, ln2_b=ln2_b,
               w_ffn_gate=w_ffn_gate.astype(BF16), w_ffn_up=w_ffn_up.astype(BF16), w_ffn_conv=w_ffn_conv,
               b_ffn_conv=b_ffn_conv, w_ffn_down=w_ffn_down.astype(BF16))
    lbt = _lb_tables(lb_param)
    yp, kkp, kvp, cbp, wkp, wvp, hgp, ffp = _trunk_prompt(x_prompt, prm, lbt)
    ys, kks, kvs, cbs, wks, wvs, hgs, ffs = _trunk_sample(
        x_sample, prm, lbt, cache_k, cache_v, page_table, state_conv_b, state_win_k, state_win_v, state_hgrn, state_ffn)
    return (yp, ys, kkp, kvp, kks, kvs, cbp, cbs, wkp, wvp, wks, wvs, hgp, hgs, ffp, ffs)
```

```python
import functools

import jax
import jax.numpy as jnp
from jax import lax
from jax.experimental import pallas as pl
from jax.experimental.pallas import tpu as pltpu

F32 = jnp.float32
BF16 = jnp.bfloat16

D_MODEL = 1024
DEPTH = 4
N_PAIR = DEPTH // 2
HEAD_DIM = 64
ROT_DIM = HEAD_DIM // 4
ROPE_THETA = 500000.0
N_HEADS = 8
N_KV = 2
GROUP = N_HEADS // N_KV
QW = N_HEADS * HEAD_DIM
KVW = N_KV * HEAD_DIM
MOBA_BLOCK = 256
MOBA_TOPK = 3
Q_CHUNK = 128
PAGE_SIZE = 128
B_CH = 512
B_CONV_WIDTH = 31
WINDOW = 128
D_HEADS = 4
D_DK = 128
D_FF = 2816
FFN_CONV_WIDTH = 3
LN_EPS = 1e-5
ALPHA = (2.0 * DEPTH) ** 0.25
SCALE = HEAD_DIM ** -0.5
SCALE_LOG2 = SCALE * 1.4426950408889634
EVEN_IN = QW + 2 * KVW + 2 * B_CH
ODD_IN = QW + 2 * KVW + 4 * 512

LANES = 128
NEG = -1e30
VMEM_LIMIT = 56 * 1024 * 1024

HIGHEST = lax.Precision.HIGHEST


def _nn(a, b, precision=None):
    return lax.dot_general(a, b, (((1,), (0,)), ((), ())), preferred_element_type=F32, precision=precision)


def _nt(a, b, precision=None):
    return lax.dot_general(a, b, (((1,), (1,)), ((), ())), preferred_element_type=F32, precision=precision)


def _ln(z, g, b):
    mu = jnp.mean(z, axis=-1, keepdims=True)
    d = z - mu
    var = jnp.mean(d * d, axis=-1, keepdims=True)
    return d * lax.rsqrt(var + LN_EPS) * g + b


def _silu(x):
    return x * jax.nn.sigmoid(x)


def _params(sem):
    return pltpu.CompilerParams(dimension_semantics=sem, vmem_limit_bytes=VMEM_LIMIT)


def _layer_spec(w, layer):
    zeros = (0,) * (w.ndim - 1)
    return pl.BlockSpec((None,) + w.shape[1:], lambda *_: (layer,) + zeros)


def _rope_tables(pos):
    half = ROT_DIM // 2
    inv = ROPE_THETA ** (-jnp.arange(half, dtype=F32) * 2.0 / ROT_DIM)
    ang = pos.astype(F32)[:, None] * inv[None, :]
    cos, sin = jnp.cos(ang), jnp.sin(ang)
    p = pos.shape[0]
    one = jnp.ones((p, HEAD_DIM - ROT_DIM), F32)
    zero = jnp.zeros((p, HEAD_DIM - ROT_DIM), F32)
    zh = jnp.zeros((p, half), F32)
    c = jnp.concatenate([cos, cos, one], 1)
    sa = jnp.concatenate([-sin, zh, zero], 1)
    sb = jnp.concatenate([zh, sin, zero], 1)
    return tuple(jnp.tile(t, (1, LANES // HEAD_DIM)) for t in (c, sa, sb))


def _rope(z, c, sa, sb):
    return z * c + pltpu.roll(z, LANES - ROT_DIM // 2, 1) * sa + pltpu.roll(z, ROT_DIM // 2, 1) * sb


def _inproj_even_body(x_ref, w_ref, c_ref, sa_ref, sb_ref, q_ref, k_ref, v_ref, u_ref):
    xb = x_ref[...].astype(BF16)
    c, sa, sb = c_ref[...], sa_ref[...], sb_ref[...]
    hq = _nn(xb, w_ref[:, 0:QW])
    for s in range(QW // LANES):
        q_ref[:, LANES * s:LANES * (s + 1)] = _rope(hq[:, LANES * s:LANES * (s + 1)], c, sa, sb)
    hkv = _nn(xb, w_ref[:, QW:QW + 2 * KVW])
    k_ref[...] = _rope(hkv[:, 0:KVW], c, sa, sb)
    v_ref[...] = hkv[:, KVW:2 * KVW]
    o = QW + 2 * KVW
    ga = _nn(xb, w_ref[:, o:o + B_CH])
    gb = _nn(xb, w_ref[:, o + B_CH:o + 2 * B_CH])
    u_ref[...] = ga * jax.nn.sigmoid(gb)


def _inproj_odd_body(x_ref, w_ref, c_ref, sa_ref, sb_ref, q_ref, k_ref, v_ref, d_ref):
    xb = x_ref[...].astype(BF16)
    c, sa, sb = c_ref[...], sa_ref[...], sb_ref[...]
    hq = _nn(xb, w_ref[:, 0:QW])
    for s in range(QW // LANES):
        q_ref[:, LANES * s:LANES * (s + 1)] = _rope(hq[:, LANES * s:LANES * (s + 1)], c, sa, sb)
    hkv = _nn(xb, w_ref[:, QW:QW + 2 * KVW])
    k_ref[...] = _rope(hkv[:, 0:KVW], c, sa, sb)
    v_ref[...] = hkv[:, KVW:2 * KVW]
    o = QW + 2 * KVW
    d_ref[:, 0:512] = _silu(_nn(xb, w_ref[:, o:o + 512]))
    d_ref[:, 512:1536] = _nn(xb, w_ref[:, o + 512:o + 1536])
    d_ref[:, 1536:2048] = _silu(_nn(xb, w_ref[:, o + 1536:o + 2048]))


def _inproj(x2, w_bf, layer, tabs, even, tm):
    m = x2.shape[0]
    ntab = tabs[0].shape[0] // tm
    row = lambda i: (i, 0)
    tab = lambda i: (i % ntab, 0)
    wide = B_CH if even else 2048
    return pl.pallas_call(
        _inproj_even_body if even else _inproj_odd_body,
        grid=(m // tm,),
        in_specs=[pl.BlockSpec((tm, D_MODEL), row), _layer_spec(w_bf, layer),
                  pl.BlockSpec((tm, LANES), tab), pl.BlockSpec((tm, LANES), tab), pl.BlockSpec((tm, LANES), tab)],
        out_specs=[pl.BlockSpec((tm, QW), row), pl.BlockSpec((tm, KVW), row), pl.BlockSpec((tm, KVW), row),
                   pl.BlockSpec((tm, wide), row)],
        out_shape=[jax.ShapeDtypeStruct((m, QW), F32), jax.ShapeDtypeStruct((m, KVW), F32),
                   jax.ShapeDtypeStruct((m, KVW), F32), jax.ShapeDtypeStruct((m, wide), F32)],
        compiler_params=_params(("parallel",)),
        name="inproj_even" if even else "inproj_odd",
    )(x2, w_bf, *tabs)


def _queries_t(q):
    return jnp.concatenate([q[:, LANES * s:LANES * (s + 1)].T for s in range(QW // LANES)], 0)


def _stack_heads_t(qt, j):
    t = qt.shape[1]
    z = jnp.zeros((HEAD_DIM, t), F32)
    blocks = []
    for g in range(GROUP):
        h = GROUP * j + g
        hd = qt[HEAD_DIM * h:HEAD_DIM * (h + 1), :]
        blocks.append(jnp.concatenate([hd, z] if j == 0 else [z, hd], 0))
    return jnp.concatenate(blocks, 1)


def _unstack_heads(outs, t):
    lane = lax.broadcasted_iota(jnp.int32, (t, LANES), 1)
    slabs = []
    for p in range(N_HEADS // 2):
        halves = []
        for hh in range(2):
            h = 2 * p + hh
            j, g = h // GROUP, h % GROUP
            blk = outs[j][g * t:(g + 1) * t]
            if j != hh:
                blk = pltpu.roll(blk, HEAD_DIM, 1)
            halves.append(blk)
        slabs.append(jnp.where(lane < HEAD_DIM, halves[0], halves[1]))
    return jnp.concatenate(slabs, 1)


def _untranspose(o_t, t):
    return jnp.concatenate([o_t[:, t * g:t * (g + 1)].T for g in range(GROUP)], 0)


def _moba_prompt_body(q_ref, k_ref, v_ref, o_ref, km_ref, vt_ref, sel_ref, *, nb):
    c = pl.program_id(1)
    blk = c // (MOBA_BLOCK // Q_CHUNK)
    qs0 = c * Q_CHUNK
    cols = N_HEADS * Q_CHUNK
    nbp = km_ref.shape[0]

    @pl.when(c == 0)
    def _():
        km_ref[...] = jnp.concatenate(
            [jnp.mean(k_ref[n * MOBA_BLOCK:(n + 1) * MOBA_BLOCK, :], axis=0, keepdims=True) for n in range(nb)]
            + [jnp.zeros((nbp - nb, LANES), F32)] * (nbp > nb), 0)
        for n in range(nb):
            vt_ref[n] = v_ref[n * MOBA_BLOCK:(n + 1) * MOBA_BLOCK, :].T.astype(BF16)

    qt = _queries_t(q_ref[...] * SCALE_LOG2)
    blk_row = lax.broadcasted_iota(jnp.int32, (nbp, cols), 0)
    kidx = lax.broadcasted_iota(jnp.int32, (MOBA_BLOCK, cols), 0)
    qpos = qs0 + lax.broadcasted_iota(jnp.int32, (MOBA_BLOCK, cols), 1) % Q_CHUNK
    qs_t = jnp.concatenate([_stack_heads_t(qt, j) for j in range(N_KV)], 1)
    qsb = qs_t.astype(BF16)
    gate = jnp.where(blk_row < blk, _nn(km_ref[...], qs_t, HIGHEST), -jnp.inf)
    sel_rows = []
    for n in range(nb):
        gn = gate[n:n + 1, :]
        beats = jnp.where((gate > gn) | ((gate == gn) & (blk_row < n)), 1.0, 0.0)
        cnt = jnp.sum(beats, axis=0, keepdims=True)
        sel_rows.append(jnp.where((cnt < MOBA_TOPK) & (n < blk), 1.0, 0.0))
    sel_ref[...] = jnp.concatenate(sel_rows + [jnp.zeros((nbp - nb, cols), F32)] * (nbp > nb), 0)

    def scores(n):
        off = pl.multiple_of(n * MOBA_BLOCK, MOBA_BLOCK)
        return _nn(k_ref[pl.ds(off, MOBA_BLOCK), :].astype(BF16), qsb), vt_ref[n]

    s, vt = scores(blk)
    s = jnp.where(kidx + blk * MOBA_BLOCK <= qpos, s, NEG)
    m = jnp.max(s, axis=0, keepdims=True)
    p = jnp.exp2(s - m)
    l = jnp.sum(p, axis=0, keepdims=True)
    acc = _nn(vt, p.astype(BF16))

    def past(i, carry):
        m, l, acc = carry
        n0 = 2 * i
        n1 = n0 + 1
        s0, vt0 = scores(n0)
        s1, vt1 = scores(n1)
        s0 = jnp.where(sel_ref[pl.ds(n0, 1), :] > 0.5, s0, NEG)
        s1 = jnp.where(sel_ref[pl.ds(n1, 1), :] > 0.5, s1, NEG)
        m_new = jnp.maximum(m, jnp.maximum(jnp.max(s0, axis=0, keepdims=True), jnp.max(s1, axis=0, keepdims=True)))
        a = jnp.exp2(m - m_new)
        p0 = jnp.exp2(s0 - m_new)
        p1 = jnp.exp2(s1 - m_new)
        l_new = a * l + jnp.sum(p0, axis=0, keepdims=True) + jnp.sum(p1, axis=0, keepdims=True)
        return m_new, l_new, a * acc + _nn(vt0, p0.astype(BF16)) + _nn(vt1, p1.astype(BF16))

    m, l, acc = lax.fori_loop(0, (blk + 1) // 2, past, (m, l, acc))
    o_t = acc / l
    half = GROUP * Q_CHUNK
    o = _unstack_heads([_untranspose(o_t[:, half * j:half * (j + 1)], Q_CHUNK) for j in range(N_KV)], Q_CHUNK)
    o_ref[...] = o.astype(o_ref.dtype)


def _moba_prompt(q, k, v):
    b, l, _ = q.shape
    nb = l // MOBA_BLOCK
    nbp = -(-nb // 8) * 8
    return pl.pallas_call(
        functools.partial(_moba_prompt_body, nb=nb),
        grid=(b, l // Q_CHUNK),
        in_specs=[pl.BlockSpec((None, Q_CHUNK, QW), lambda i, c: (i, c, 0)),
                  pl.BlockSpec((None, l, KVW), lambda i, c: (i, 0, 0)),
                  pl.BlockSpec((None, l, KVW), lambda i, c: (i, 0, 0))],
        out_specs=pl.BlockSpec((None, Q_CHUNK, QW), lambda i, c: (i, c, 0)),
        out_shape=jax.ShapeDtypeStruct((b, l, QW), BF16),
        scratch_shapes=[pltpu.VMEM((nbp, LANES), F32), pltpu.VMEM((nb, KVW, MOBA_BLOCK), BF16),
                        pltpu.VMEM((nbp, N_HEADS * Q_CHUNK), F32)],
        compiler_params=_params(("parallel", "arbitrary")),
        name="moba_prompt",
    )(q, k, v)


def _stack_decode(q):
    s = q.shape[0]
    lane = lax.broadcasted_iota(jnp.int32, (s, LANES), 1)
    parts = []
    for h in range(N_HEADS):
        j = h // GROUP
        slab = q[:, LANES * (h // 2):LANES * (h // 2 + 1)]
        if h % 2 != j:
            slab = pltpu.roll(slab, HEAD_DIM, 1)
        keep = (lane < HEAD_DIM) if j == 0 else (lane >= HEAD_DIM)
        parts.append(jnp.where(keep, slab, 0.0))
    return jnp.concatenate(parts, 0)


def _unstack_decode(o, s):
    lane = lax.broadcasted_iota(jnp.int32, (s, LANES), 1)
    slabs = []
    for p in range(N_HEADS // 2):
        halves = []
        for hh in range(2):
            h = 2 * p + hh
            blk = o[h * s:(h + 1) * s]
            if h // GROUP != hh:
                blk = pltpu.roll(blk, HEAD_DIM, 1)
            halves.append(blk)
        slabs.append(jnp.where(lane < HEAD_DIM, halves[0], halves[1]))
    return jnp.concatenate(slabs, 1)


def _pad_rows(x, n):
    return jnp.concatenate([x, jnp.zeros((n - x.shape[0], x.shape[1]), x.dtype)], 0)


def _kv_transposed(x):
    lead = x.shape[:-3]
    n = len(lead)
    t = x.shape[-3]
    return jnp.transpose(x, tuple(range(n)) + (n + 1, n + 2, n)).reshape(lead + (KVW, t))


def _kv_untransposed(xt):
    lead = xt.shape[:-2]
    n = len(lead)
    t = xt.shape[-1]
    return jnp.transpose(xt.reshape(lead + (N_KV, HEAD_DIM, t)), tuple(range(n)) + (n + 2, n, n + 1))


def _moba_sample_body(pt_ref, q_ref, kn_ref, vn_ref, kpool_ref, vpool_ref, o_ref, kbuf_ref, vbuf_ref, kb_ref, s_ref,
                      sem_ref, *, nb, s_len, layer, nbatch):
    ppb = MOBA_BLOCK // PAGE_SIZE
    assert ppb == 2 and 2 * N_HEADS * s_len == LANES
    npages = nb * ppb
    rows = N_HEADS * s_len
    nbp = -(-nb // 8) * 8
    b = pl.program_id(0)
    slot = b % 2
    nslot = 1 - slot
    nxt = jnp.minimum(b + 1, nbatch - 1)

    def k_copy(bi, p, sl):
        return pltpu.make_async_copy(kpool_ref.at[layer, pt_ref[bi, p]], kbuf_ref.at[sl, p], sem_ref.at[sl, 0])

    def v_copy(bi, p, sl):
        return pltpu.make_async_copy(vpool_ref.at[layer, pt_ref[bi, p]], vbuf_ref.at[sl, p], sem_ref.at[sl, 1])

    @pl.when(b == 0)
    def _():
        for p in range(npages):
            k_copy(0, p, 0).start()
            v_copy(0, p, 0).start()

    for p in range(npages):
        k_copy(b, p, slot).wait()
        v_copy(b, p, slot).wait()

    qa = _stack_decode(q_ref[...] * SCALE_LOG2)
    qt = jnp.concatenate([qa, qa], 0).T
    lane = lax.broadcasted_iota(jnp.int32, (LANES, LANES), 1)
    lo = lane < rows
    qt_lo = jnp.where(lo, qt, 0.0).astype(BF16)
    qt_hi = jnp.where(lo, 0.0, qt).astype(BF16)
    sums = []
    for p in range(npages):
        k_copy(nxt, p, nslot).start()
        v_copy(nxt, p, nslot).start()
        kpg = kbuf_ref[slot, p].T
        kb_ref[p] = kpg.astype(BF16)
        sums.append(jnp.sum(kpg, axis=0, keepdims=True))
    km = jnp.concatenate([sum(sums[ppb * n:ppb * (n + 1)]) * (1.0 / MOBA_BLOCK) for n in range(nb)]
                         + [jnp.zeros((nbp - nb, LANES), F32)] * (nbp > nb), 0)
    blk_row = lax.broadcasted_iota(jnp.int32, (nbp, LANES), 0)
    gate = jnp.where(blk_row < nb, _nn(km, qt, HIGHEST), -jnp.inf)
    sel = []
    for n in range(nb):
        gn = gate[n:n + 1, :]
        beats = jnp.where((gate > gn) | ((gate == gn) & (blk_row < n)), 1.0, 0.0)
        sel.append(jnp.sum(beats, axis=0, keepdims=True) < MOBA_TOPK)
    kidx = lax.broadcasted_iota(jnp.int32, (LANES, LANES), 0)
    tok = lane % s_len
    s_new = jnp.where((kidx <= tok) & (kidx < s_len) & lo, _nn(_pad_rows(kn_ref[...], LANES).astype(BF16), qt_lo), NEG)
    m = jnp.max(s_new, axis=0, keepdims=True)
    for n in range(nb):
        s = jnp.where(sel[n], _nn(kb_ref[ppb * n], qt_lo) + _nn(kb_ref[ppb * n + 1], qt_hi), NEG)
        s_ref[n] = s
        m = jnp.maximum(m, jnp.max(s, axis=0, keepdims=True))
    m = jnp.maximum(m, pltpu.roll(m, rows, 1))
    p_new = jnp.exp2(s_new - m)
    l = jnp.sum(p_new, axis=0, keepdims=True)
    acc_lo =_nn(_pad_rows(vn_ref[...], LANES).T.astype(BF16), p_new.astype(BF16))
    acc_hi = jnp.zeros((KVW, LANES), F32)
    for n in range(nb):
        pp = jnp.exp2(s_ref[n] - m)
        l = l + jnp.sum(pp, axis=0, keepdims=True)
        ppb16 = pp.astype(BF16)
        acc_lo = acc_lo + _nn(vbuf_ref[slot, ppb * n].astype(BF16), ppb16)
        acc_hi = acc_hi + _nn(vbuf_ref[slot, ppb * n + 1].astype(BF16), ppb16)
    acc = jnp.where(lo, acc_lo, acc_hi)
    o_t = (acc + pltpu.roll(acc, rows, 1)) / (l + pltpu.roll(l, rows, 1))
    o_ref[...] = _unstack_decode(o_t.T[0:rows], s_len)

    @pl.when(b == nbatch - 1)
    def _():
        for p in range(npages):
            k_copy(nxt, p, nslot).wait()
            v_copy(nxt, p, nslot).wait()


def _moba_sample(q, k_new, v_new, k_pool_t, v_pool_t, page_table, layer):
    b, s_len, _ = q.shape
    npages = page_table.shape[1]
    assert (npages * PAGE_SIZE) % MOBA_BLOCK == 0 and N_HEADS * s_len <= LANES
    nb = npages * PAGE_SIZE // MOBA_BLOCK
    assert 0 < nb <= LANES

    tok = lambda i, pt: (i, 0, 0)
    page_buf = pltpu.VMEM((2, npages, KVW, PAGE_SIZE), F32)
    grid_spec = pltpu.PrefetchScalarGridSpec(
        num_scalar_prefetch=1,
        grid=(b,),
        in_specs=[pl.BlockSpec((None, s_len, QW), tok), pl.BlockSpec((None, s_len, KVW), tok),
                  pl.BlockSpec((None, s_len, KVW), tok),
                  pl.BlockSpec(memory_space=pl.ANY), pl.BlockSpec(memory_space=pl.ANY)],
        out_specs=pl.BlockSpec((None, s_len, QW), tok),
        scratch_shapes=[page_buf, page_buf, pltpu.VMEM((npages, PAGE_SIZE, KVW), BF16),
                        pltpu.VMEM((nb, PAGE_SIZE, LANES), F32), pltpu.SemaphoreType.DMA((2, 2))],
    )
    return pl.pallas_call(
        functools.partial(_moba_sample_body, nb=nb, s_len=s_len, layer=layer, nbatch=b),
        grid_spec=grid_spec,
        out_shape=jax.ShapeDtypeStruct((b, s_len, QW), F32),
        compiler_params=_params(("arbitrary",)),
        name="moba_sample",
    )(page_table, q, k_new, v_new, k_pool_t, v_pool_t)


def _swa_prompt_body(q_ref, kp_ref, kc_ref, vp_ref, vc_ref, sink_ref, o_ref, *, nw):
    t = pl.program_id(1)
    cols = N_HEADS * WINDOW
    kidx = lax.broadcasted_iota(jnp.int32, (2 * WINDOW, cols), 0)
    tq = lax.broadcasted_iota(jnp.int32, (2 * WINDOW, cols), 1) % WINDOW
    band = (kidx >= tq) & (kidx <= tq + WINDOW)
    sink = sink_ref[...]
    half = GROUP * WINDOW
    kblk = [kp_ref[...]] + [kc_ref[WINDOW * w:WINDOW * (w + 1), :] for w in range(nw)]
    vblk_t = [vp_ref[...].T] + [vc_ref[WINDOW * w:WINDOW * (w + 1), :].T for w in range(nw)]
    for w in range(nw):
        qt = _queries_t(q_ref[WINDOW * w:WINDOW * (w + 1), :] * SCALE)
        kk = jnp.concatenate(kblk[w:w + 2], 0).astype(BF16)
        vv_t = jnp.concatenate(vblk_t[w:w + 2], 1).astype(BF16)
        valid = band & (kidx >= jnp.where(t == 0, WINDOW, 0)) if w == 0 else band
        qs_t = jnp.concatenate([_stack_heads_t(qt, j) for j in range(N_KV)], 1)
        s = jnp.where(valid, _nn(kk, qs_t.astype(BF16)), NEG)
        m = jnp.maximum(jnp.max(s, axis=0, keepdims=True), sink)
        p = jnp.exp(s - m)
        den = jnp.sum(p, axis=0, keepdims=True) + jnp.exp(sink - m)
        o_t = _nn(vv_t, p.astype(BF16)) / den
        o = _unstack_heads([_untranspose(o_t[:, half * j:half * (j + 1)], WINDOW) for j in range(N_KV)], WINDOW)
        o_ref[WINDOW * w:WINDOW * (w + 1), :] = o.astype(o_ref.dtype)


def _swa_prompt(q, k, v, sink, nw=4):
    b, l, _ = q.shape
    assert l % (nw * WINDOW) == 0
    sink_cols = jnp.repeat(sink.astype(F32), WINDOW)[None, :]
    cur = lambda i, t: (i, t, 0)
    prev = lambda i, t: (i, jnp.maximum(nw * t - 1, 0), 0)
    return pl.pallas_call(
        functools.partial(_swa_prompt_body, nw=nw),
        grid=(b, l // (nw * WINDOW)),
        in_specs=[pl.BlockSpec((None, nw * WINDOW, QW), cur),
                  pl.BlockSpec((None, WINDOW, KVW), prev), pl.BlockSpec((None, nw * WINDOW, KVW), cur),
                  pl.BlockSpec((None, WINDOW, KVW), prev), pl.BlockSpec((None, nw * WINDOW, KVW), cur),
                  pl.BlockSpec((1, N_HEADS * WINDOW), lambda i, t: (0, 0))],
        out_specs=pl.BlockSpec((None, nw * WINDOW, QW), cur),
        out_shape=jax.ShapeDtypeStruct((b, l, QW), BF16),
        compiler_params=_params(("parallel", "arbitrary")),
        name="swa_prompt",
    )(q, k, k, v, v, sink_cols)


def _swa_sample_body(q_ref, kn_ref, vn_ref, wk_ref, wv_ref, sink_ref, o_ref, nk_ref, nv_ref, *, bt, s_len):
    rows = N_HEADS * s_len
    tok = lax.broadcasted_iota(jnp.int32, (rows, LANES), 0) % s_len
    lane = lax.broadcasted_iota(jnp.int32, (rows, LANES), 1)
    lane_w = lax.broadcasted_iota(jnp.int32, (KVW, WINDOW), 1)
    sink = sink_ref[...]

    def one(i, carry):
        qab = _stack_decode(q_ref[i] * SCALE).astype(BF16)
        kn = _pad_rows(kn_ref[i], LANES)
        vn = _pad_rows(vn_ref[i], LANES)
        wk_t = wk_ref[i]
        wv_t = wv_ref[i]
        s_win = jnp.where(lane >= tok, _nn(qab, wk_t.astype(BF16)), NEG)
        s_new = jnp.where((lane <= tok) & (lane < s_len), _nt(qab, kn.astype(BF16)), NEG)
        m = jnp.maximum(jnp.maximum(jnp.max(s_win, axis=1, keepdims=True), jnp.max(s_new, axis=1, keepdims=True)), sink)
        p_win = jnp.exp(s_win - m)
        p_new = jnp.exp(s_new - m)
        den = jnp.sum(p_win, axis=1, keepdims=True) + jnp.sum(p_new, axis=1, keepdims=True) + jnp.exp(sink - m)
        acc = _nt(p_win.astype(BF16), wv_t.astype(BF16)) + _nn(p_new.astype(BF16), vn.astype(BF16))
        o_ref[i] = _unstack_decode(acc / den, s_len)
        keep = lane_w < WINDOW - s_len
        nk_ref[i] = jnp.where(keep, pltpu.roll(wk_t, WINDOW - s_len, 1), pltpu.roll(kn.T, WINDOW - s_len, 1))
        nv_ref[i] = jnp.where(keep, pltpu.roll(wv_t, WINDOW - s_len, 1), pltpu.roll(vn.T, WINDOW - s_len, 1))
        return carry

    def pair(i, carry):
        return one(2 * i + 1, one(2 * i, carry))

    assert bt % 2 == 0
    lax.fori_loop(0, bt // 2, pair, 0)


def _swa_sample(q, k_new, v_new, win_k_t, win_v_t, layer, sink, bt=16):
    b, s_len, _ = q.shape
    bt = min(bt, b)
    assert win_k_t.shape[3] == WINDOW == LANES and b % bt == 0 and s_len <= WINDOW
    sink_rows = jnp.repeat(sink.astype(F32), s_len)[:, None]
    blk = lambda i: (i, 0, 0)
    win = pl.BlockSpec((bt, KVW, WINDOW), blk)
    win_in = pl.BlockSpec((None, bt, KVW, WINDOW), lambda i: (layer, i, 0, 0))
    return pl.pallas_call(
        functools.partial(_swa_sample_body, bt=bt, s_len=s_len),
        grid=(b // bt,),
        in_specs=[pl.BlockSpec((bt, s_len, QW), blk), pl.BlockSpec((bt, s_len, KVW), blk),
                  pl.BlockSpec((bt, s_len, KVW), blk), win_in, win_in,
                  pl.BlockSpec((N_HEADS * s_len, 1), lambda i: (0, 0))],
        out_specs=[pl.BlockSpec((bt, s_len, QW), blk), win, win],
        out_shape=[jax.ShapeDtypeStruct((b, s_len, QW), F32), jax.ShapeDtypeStruct((b, KVW, WINDOW), F32),
                   jax.ShapeDtypeStruct((b, KVW, WINDOW), F32)],
        compiler_params=_params(("parallel",)),
        name="swa_sample",
    )(q, k_new, v_new, win_k_t, win_v_t, sink_rows)


def _conv_prompt_body(prev_ref, cur_ref, w_ref, b_ref, g_ref, bb_ref, o_ref, ext_ref, sh_ref, raw_ref, *, tm, halo):
    t = pl.program_id(1)
    ext_ref[0:halo, :] = jnp.where(t == 0, 0.0, prev_ref[...])
    ext_ref[halo:halo + tm, :] = cur_ref[...]
    kw = w_ref.shape[0]
    base = halo - (kw - 1)
    sub = 8
    for lc in range(B_CH // LANES):
        ln = slice(LANES * lc, LANES * (lc + 1))
        acc = jnp.zeros((tm, LANES), F32) + b_ref[:, ln]
        for r in range(sub):
            offs = [o for o in range(base, base + kw) if o % sub == r]
            if not offs:
                continue
            n = tm + max(offs) - r
            if r:
                sh_ref[0:n, :] = ext_ref[pl.ds(r, n), ln]
            src = sh_ref if r else ext_ref.at[:, ln]
            for o in offs:
                acc = acc + w_ref[o - base:o - base + 1, ln] * src[pl.ds(o - r, tm), :]
        raw_ref[:, ln] = acc
    o_ref[...] = _silu(_ln(raw_ref[...], g_ref[...], bb_ref[...])).astype(o_ref.dtype)


def _conv_prompt(u, w, b, g, bb, tm=512, halo=32):
    bsz, l, _ = u.shape
    r = tm // halo
    vec = lambda i, t: (0, 0)
    return pl.pallas_call(
        functools.partial(_conv_prompt_body, tm=tm, halo=halo),
        grid=(bsz, l // tm),
        in_specs=[pl.BlockSpec((None, halo, B_CH), lambda i, t: (i, jnp.maximum(t * r - 1, 0), 0)),
                  pl.BlockSpec((None, tm, B_CH), lambda i, t: (i, t, 0)),
                  pl.BlockSpec(w.shape, vec), pl.BlockSpec((1, B_CH), vec), pl.BlockSpec((1, B_CH), vec),
                  pl.BlockSpec((1, B_CH), vec)],
        out_specs=pl.BlockSpec((None, tm, B_CH), lambda i, t: (i, t, 0)),
        out_shape=jax.ShapeDtypeStruct((bsz, l, B_CH), BF16),
        scratch_shapes=[pltpu.VMEM((tm + halo, B_CH), F32), pltpu.VMEM((tm + halo, LANES), F32),
                        pltpu.VMEM((tm, B_CH), F32)],
        compiler_params=_params(("parallel", "arbitrary")),
        name="conv_prompt",
    )(u, u, w, b[None], g[None], bb[None])


def _conv_sample_body(up_ref, w_ref, b_ref, g_ref, bb_ref, o_ref, *, bt, s_len):
    kw = w_ref.shape[0]
    acc = jnp.zeros((bt, s_len, B_CH), F32) + b_ref[...]
    for j in range(kw):
        acc = acc + w_ref[j:j + 1, :] * up_ref[:, pl.ds(j, s_len), :]
    o_ref[...] = _silu(_ln(acc, g_ref[...], bb_ref[...]))


def _conv_sample(up, w, b, g, bb, bt=32):
    bsz, rows, _ = up.shape
    bt = min(bt, bsz)
    s_len = rows - (w.shape[0] - 1)
    vec = lambda i: (0, 0)
    return pl.pallas_call(
        functools.partial(_conv_sample_body, bt=bt, s_len=s_len),
        grid=(bsz // bt,),
        in_specs=[pl.BlockSpec((bt, rows, B_CH), lambda i: (i, 0, 0)),
                  pl.BlockSpec(w.shape, vec), pl.BlockSpec((1, B_CH), vec), pl.BlockSpec((1, B_CH), vec),
                  pl.BlockSpec((1, B_CH), vec)],
        out_specs=pl.BlockSpec((bt, s_len, B_CH), lambda i: (i, 0, 0)),
        out_shape=jax.ShapeDtypeStruct((bsz, s_len, B_CH), F32),
        compiler_params=_params(("parallel",)),
        name="conv_sample",
    )(up, w, b[None], g[None], bb[None])


def _hgrn_tile(hq, df, v, lb, ch):
    n = LANES // ch
    loglb, log1mlb, onemlb = lb[0:1, :], lb[1:2, :], lb[2:3, :]
    e = jnp.exp(-jnp.abs(df))
    ls = jnp.minimum(df, 0.0) - jnp.log1p(e)
    b_ = log1mlb + ls
    mx = jnp.maximum(loglb, b_)
    logf = mx + jnp.log(jnp.exp(loglb - mx) + jnp.exp(b_ - mx))
    hk = onemlb * (jnp.where(df >= 0.0, e, 1.0) / (1.0 + e))
    row = lax.broadcasted_iota(jnp.int32, (LANES, LANES), 0)
    colv = lax.broadcasted_iota(jnp.int32, (LANES, LANES), 1)
    r = row % ch
    g = logf
    sh = 1
    while sh < ch:
        g = g + jnp.where(r >= sh, pltpu.roll(g, sh, 0), 0.0)
        sh *= 2
    qa, kb, kd, qe, dec = [], [], [], [], []
    for c in range(n):
        sl = slice(ch * c, ch * (c + 1))
        gc = g[sl]
        gm = gc[ch // 2 - 1:ch // 2]
        gl = gc[ch - 1:ch]
        qa.append(hq[sl] * jnp.exp(gc - gm))
        kb.append(hk[sl] * jnp.exp(gm - gc))
        kd.append(hk[sl] * jnp.exp(gl - gc))
        qe.append((hq[sl] * jnp.exp(gc)).astype(BF16))
        dec.append(jnp.exp(gl))
    att = _nt(jnp.concatenate(qa, 0).astype(BF16), jnp.concatenate(kb, 0).astype(BF16))
    att = jnp.where((row // ch == colv // ch) & (colv <= row), att, 0.0)
    vb = v.astype(BF16)
    o_intra = _nn(att.astype(BF16), vb)
    vt = v.T
    kdb = jnp.concatenate(kd, 0).astype(BF16)
    ut = [_nn(jnp.where(colv // ch == c, vt, 0.0).astype(BF16), kdb) for c in range(n)]
    return o_intra, ut, qe, dec


def _hgrn_finish(o, sg, gn):
    return o * lax.rsqrt(jnp.mean(o * o, axis=-1, keepdims=True) + LN_EPS) * gn * sg


def _hgrn_prompt_body(hq_ref, df_ref, di_ref, sg_ref, lb_ref, gn_ref, o_ref, s_ref, *, ngroups, ch, nh):
    def group(gi, sts):
        off = pl.multiple_of(gi * LANES, LANES)
        rows = pl.ds(off, LANES)
        new = []
        for hh in range(nh):
            ln = slice(LANES * hh, LANES * (hh + 1))
            st = sts[hh]
            o_intra, ut, qe, dec = _hgrn_tile(hq_ref[rows, ln], df_ref[rows, ln], di_ref[rows, ln], lb_ref[:, ln], ch)
            o_inter = []
            for c in range(LANES // ch):
                o_inter.append(_nt(qe[c], st.astype(BF16)))
                st = dec[c] * st + ut[c]
            o = o_intra + jnp.concatenate(o_inter, 0)
            o_ref[rows, ln] = _hgrn_finish(o, sg_ref[rows, ln], gn_ref[...]).astype(o_ref.dtype)
            new.append(st)
        return tuple(new)

    zero = jnp.zeros((LANES, LANES), F32)
    sts = lax.fori_loop(0, ngroups, group, (zero,) * nh)
    for hh in range(nh):
        s_ref[hh] = sts[hh].T


def _hgrn_prompt(d4, lbt, gn, ch=16, nh=4):
    b, l, _ = d4.shape
    hp = D_HEADS // nh
    seg = lambda s: pl.BlockSpec((None, l, nh * LANES), lambda i, h: (i, 0, hp * s + h))
    return pl.pallas_call(
        functools.partial(_hgrn_prompt_body, ngroups=l // LANES, ch=ch, nh=nh),
        grid=(b, hp),
        in_specs=[seg(0), seg(1), seg(2), seg(3),
                  pl.BlockSpec((3, nh * LANES), lambda i, h: (0, h)), pl.BlockSpec((1, LANES), lambda i, h: (0, 0))],
        out_specs=[pl.BlockSpec((None, l, nh * LANES), lambda i, h: (i, 0, h)),
                   pl.BlockSpec((None, nh, D_DK, LANES), lambda i, h: (i, h, 0, 0))],
        out_shape=[jax.ShapeDtypeStruct((b, l, D_HEADS * LANES), BF16),
                   jax.ShapeDtypeStruct((b, D_HEADS, D_DK, LANES), F32)],
        compiler_params=_params(("parallel", "arbitrary")),
        name="hgrn_prompt",
    )(d4, d4, d4, d4, lbt, gn[None])


def _hgrn_sample_body(hq_ref, df_ref, di_ref, sg_ref, lb_ref, gn_ref, s0_ref, o_ref, s_ref, *, ch):
    o_intra, ut, qe, dec = _hgrn_tile(hq_ref[...], df_ref[...], di_ref[...], lb_ref[...], ch)
    o_inter = []
    for c in range(LANES // ch):
        st = s0_ref[c].T
        o_inter.append(_nt(qe[c], st.astype(BF16)))
        s_ref[c] = (dec[c] * st + ut[c]).T
    o = o_intra + jnp.concatenate(o_inter, 0)
    o_ref[...] = _hgrn_finish(o, sg_ref[...], gn_ref[...])


def _hgrn_sample(d4, lbt, gn, s0, layer):
    m = d4.shape[0]
    b = s0.shape[1]
    s_len = m // b
    assert LANES % s_len == 0 and s_len & (s_len - 1) == 0 and m % LANES == 0
    bt = LANES // s_len
    seg = lambda s: pl.BlockSpec((LANES, LANES), lambda i, h: (i, D_HEADS * s + h))
    st_in = pl.BlockSpec((None, bt, None, D_DK, LANES), lambda i, h: (layer, i, h, 0, 0))
    st_spec = pl.BlockSpec((bt, None, D_DK, LANES), lambda i, h: (i, h, 0, 0))
    return pl.pallas_call(
        functools.partial(_hgrn_sample_body, ch=s_len),
        grid=(m // LANES, D_HEADS),
        in_specs=[seg(0), seg(1), seg(2), seg(3),
                  pl.BlockSpec((3, LANES), lambda i, h: (0, h)), pl.BlockSpec((1, LANES), lambda i, h: (0, 0)), st_in],
        out_specs=[pl.BlockSpec((LANES, LANES), lambda i, h: (i, h)), st_spec],
        out_shape=[jax.ShapeDtypeStruct((m, D_HEADS * LANES), F32), jax.ShapeDtypeStruct(s0.shape[1:], F32)],
        compiler_params=_params(("parallel", "arbitrary")),
        name="hgrn_sample",
    )(d4, d4, d4, d4, lbt, gn[None], s0)


def _outproj_body(a_ref, c_ref, x_ref, w_ref, g_ref, b_ref, o_ref):
    wa = a_ref.shape[1]
    y = _nn(a_ref[...].astype(BF16), w_ref[0:wa, :]) + _nn(c_ref[...].astype(BF16), w_ref[wa:, :])
    o_ref[...] = _ln(ALPHA * x_ref[...] + y, g_ref[...], b_ref[...])


def _outproj_ln(a, c, x2, w_bf, layer, g, b, tm):
    m = x2.shape[0]
    row = lambda i: (i, 0)
    vec = lambda i: (0, 0)
    return pl.pallas_call(
        _outproj_body,
        grid=(m // tm,),
        in_specs=[pl.BlockSpec((tm, a.shape[1]), row), pl.BlockSpec((tm, c.shape[1]), row),
                  pl.BlockSpec((tm, D_MODEL), row), _layer_spec(w_bf, layer),
                  pl.BlockSpec((1, D_MODEL), vec), pl.BlockSpec((1, D_MODEL), vec)],
        out_specs=pl.BlockSpec((tm, D_MODEL), row),
        out_shape=jax.ShapeDtypeStruct((m, D_MODEL), F32),
        compiler_params=_params(("parallel",)),
        name="outproj_ln",
    )(a, c, x2, w_bf, g[None], b[None])


def _ffn_body(x_ref, hist_ref, wg_ref, wu_ref, wd_ref, wc_ref, bc_ref, g_ref, b_ref,
              o_ref, st_ref, *carry, tm, seg, fc):
    x = x_ref[...]
    xb = x.astype(BF16)
    row = lax.broadcasted_iota(jnp.int32, (tm, fc), 0)
    r = row % seg
    if seg == tm:
        carry_ref, = carry

        @pl.when(pl.program_id(1) == 0)
        def _():
            carry_ref[...] = hist_ref[...]

    acc = jnp.zeros((tm, D_MODEL), F32)
    for c in range(D_FF // fc):
        cols = slice(fc * c, fc * (c + 1))
        gp = _nn(xb, wg_ref[:, cols])
        if seg == tm:
            h0 = carry_ref[0:1, cols]
            h1 = carry_ref[1:2, cols]
            p1 = jnp.where(row == 0, h1, pltpu.roll(gp, 1, 0))
            p2 = jnp.where(row == 0, h0, jnp.where(row == 1, h1, pltpu.roll(gp, 2, 0)))
            carry_ref[:, cols] = gp[tm - 2:tm]
            st_ref[:, cols] = gp[tm - 2:tm]
        else:
            hp = hist_ref[:, cols]
            p1 = jnp.where(r == 0, pltpu.roll(hp, (tm - (seg - 1)) % tm, 0), pltpu.roll(gp, 1, 0))
            p2 = jnp.where(r < 2, pltpu.roll(hp, (tm - (seg - 2)) % tm, 0), pltpu.roll(gp, 2, 0))
            st_ref[:, cols] = gp
        cv = wc_ref[0:1, cols] * p2 + wc_ref[1:2, cols] * p1 + wc_ref[2:3, cols] * gp + bc_ref[:, cols]
        hid = 0.5 * cv * (1.0 + lax.erf(cv * (2.0 ** -0.5))) * _nn(xb, wu_ref[:, cols])
        acc = acc + _nn(hid.astype(BF16), wd_ref[cols, :])
    o_ref[...] = _ln(ALPHA * x + acc, g_ref[...], b_ref[...])


def _ffn_prompt(x, hist, wg, wu, wd, layer, wc, bc, g, b, tm=1024, fc=256):
    bsz, l, _ = x.shape
    tm = min(tm, l)
    assert l % tm == 0
    vec = lambda i, t: (0, 0)
    return pl.pallas_call(
        functools.partial(_ffn_body, tm=tm, seg=tm, fc=fc),
        grid=(bsz, l // tm),
        in_specs=[pl.BlockSpec((None, tm, D_MODEL), lambda i, t: (i, t, 0)),
                  pl.BlockSpec((None, 2, D_FF), lambda i, t: (i, 0, 0)),
                  _layer_spec(wg, layer), _layer_spec(wu, layer), _layer_spec(wd, layer),
                  pl.BlockSpec(wc.shape, vec), pl.BlockSpec((1, D_FF), vec),
                  pl.BlockSpec((1, D_MODEL), vec), pl.BlockSpec((1, D_MODEL), vec)],
        out_specs=[pl.BlockSpec((None, tm, D_MODEL), lambda i, t: (i, t, 0)),
                   pl.BlockSpec((None, 2, D_FF), lambda i, t: (i, 0, 0))],
        out_shape=[jax.ShapeDtypeStruct((bsz, l, D_MODEL), F32), jax.ShapeDtypeStruct((bsz, 2, D_FF), F32)],
        scratch_shapes=[pltpu.VMEM((FFN_CONV_WIDTH - 1, D_FF), F32)],
        compiler_params=_params(("parallel", "arbitrary")),
        name="ffn_prompt",
    )(x, hist, wg, wu, wd, wc, bc[None], g[None], b[None])


def _ffn_sample(x2, hist_rows, s_len, wg, wu, wd, layer, wc, bc, g, b, tm=512, fc=256):
    m = x2.shape[0]
    tm = min(tm, m)
    vec = lambda i: (0, 0)
    row = lambda i: (i, 0)
    return pl.pallas_call(
        functools.partial(_ffn_body, tm=tm, seg=s_len, fc=fc),
        grid=(m // tm,),
        in_specs=[pl.BlockSpec((tm, D_MODEL), row), pl.BlockSpec((tm, D_FF), row),
                  _layer_spec(wg, layer), _layer_spec(wu, layer), _layer_spec(wd, layer),
                  pl.BlockSpec(wc.shape, vec), pl.BlockSpec((1, D_FF), vec),
                  pl.BlockSpec((1, D_MODEL), vec), pl.BlockSpec((1, D_MODEL), vec)],
        out_specs=[pl.BlockSpec((tm, D_MODEL), row), pl.BlockSpec((tm, D_FF), row)],
        out_shape=[jax.ShapeDtypeStruct((m, D_MODEL), F32), jax.ShapeDtypeStruct((m, D_FF), F32)],
        compiler_params=_params(("parallel",)),
        name="ffn_sample",
    )(x2, hist_rows, wg, wu, wd, wc, bc[None], g[None], b[None])


def _lb_tables(lb_param):
    lbs = jnp.cumsum(jax.nn.softmax(lb_param.astype(F32), axis=0), axis=0)
    lbs = lbs - lbs[:1]
    return jnp.stack([jnp.log(lbs), jnp.log1p(-lbs), 1.0 - lbs], axis=1)


def _trunk_prompt(x, prm, lbt):
    b, l, _ = x.shape
    m = b * l
    tm = 512
    tabs = _rope_tables(jnp.arange(l))
    kv_k, kv_v, conv_b, win_k, win_v, hgrn, ffn = [], [], [], [], [], [], []
    zero_hist = jnp.zeros((b, FFN_CONV_WIDTH - 1, D_FF), F32)
    for layer in range(DEPTH):
        i = layer // 2
        x2 = x.reshape(m, D_MODEL)
        if layer % 2 == 0:
            q, k, v, u = _inproj(x2, prm["w_in_even"], i, tabs, True, tm)
            k3, v3, u3 = k.reshape(b, l, KVW), v.reshape(b, l, KVW), u.reshape(b, l, B_CH)
            a = _moba_prompt(q.reshape(b, l, QW), k3, v3)
            cv = _conv_prompt(u3, prm["w_dw_b"][i], prm["b_dw_b"][i], prm["conv_ln_g"][i], prm["conv_ln_b"][i])
            kv_k.append(k3.reshape(b, l, N_KV, HEAD_DIM))
            kv_v.append(v3.reshape(b, l, N_KV, HEAD_DIM))
            conv_b.append(u3[:, l - (B_CONV_WIDTH - 1):])
            x1 = _outproj_ln(a.reshape(m, QW), cv.reshape(m, B_CH), x2, prm["w_out_even"], i,
                             prm["ln1_g"][layer], prm["ln1_b"][layer], tm)
        else:
            q, k, v, d4 = _inproj(x2, prm["w_in_odd"], i, tabs, False, tm)
            k3, v3 = k.reshape(b, l, KVW), v.reshape(b, l, KVW)
            a = _swa_prompt(q.reshape(b, l, QW), k3, v3, prm["sinks"][i])
            o, st = _hgrn_prompt(d4.reshape(b, l, 2048), lbt[i], prm["gnorm_g"][i])
            keep = min(WINDOW, l)
            win_k.append(k3[:, l - keep:].reshape(b, keep, N_KV, HEAD_DIM))
            win_v.append(v3[:, l - keep:].reshape(b, keep, N_KV, HEAD_DIM))
            hgrn.append(st)
            x1 = _outproj_ln(a.reshape(m, QW), o.reshape(m, 512), x2, prm["w_out_odd"], i,
                             prm["ln1_g"][layer], prm["ln1_b"][layer], tm)
        xo, fb = _ffn_prompt(x1.reshape(b, l, D_MODEL), zero_hist, prm["w_ffn_gate"], prm["w_ffn_up"],
                             prm["w_ffn_down"], layer, prm["w_ffn_conv"][layer], prm["b_ffn_conv"][layer],
                             prm["ln2_g"][layer], prm["ln2_b"][layer])
        ffn.append(fb)
        x = xo
    return (x, jnp.stack(kv_k), jnp.stack(kv_v), jnp.stack(conv_b), jnp.stack(win_k), jnp.stack(win_v),
            jnp.stack(hgrn), jnp.stack(ffn))


def _trunk_sample(x, prm, lbt, cache_k, cache_v, page_table, state_conv_b, state_win_k, state_win_v,
                  state_hgrn, state_ffn):
    b, s, _ = x.shape
    m = b * s
    tm = min(512, m)
    past = page_table.shape[1] * PAGE_SIZE
    assert state_win_k.shape[2] == WINDOW and past + s >= WINDOW
    tabs = _rope_tables(jnp.tile(past + jnp.arange(s), tm // s))
    pool_k_t = _kv_transposed(cache_k)
    pool_v_t = _kv_transposed(cache_v)
    win_k_t = _kv_transposed(state_win_k)
    win_v_t = _kv_transposed(state_win_v)
    kv_k, kv_v, conv_b, win_k, win_v, hgrn, ffn = [], [], [], [], [], [], []
    for layer in range(DEPTH):
        i = layer // 2
        x2 = x.reshape(m, D_MODEL)
        if layer % 2 == 0:
            q, k, v, u = _inproj(x2, prm["w_in_even"], i, tabs, True, tm)
            k3, v3 = k.reshape(b, s, KVW), v.reshape(b, s, KVW)
            a = _moba_sample(q.reshape(b, s, QW), k3, v3, pool_k_t, pool_v_t, page_table, i)
            up = jnp.concatenate([state_conv_b[i], u.reshape(b, s, B_CH)], 1)
            cv = _conv_sample(up, prm["w_dw_b"][i], prm["b_dw_b"][i], prm["conv_ln_g"][i], prm["conv_ln_b"][i])
            kv_k.append(k3.reshape(b, s, N_KV, HEAD_DIM))
            kv_v.append(v3.reshape(b, s, N_KV, HEAD_DIM))
            conv_b.append(up[:, up.shape[1] - (B_CONV_WIDTH - 1):])
            x1 = _outproj_ln(a.reshape(m, QW), cv.reshape(m, B_CH), x2, prm["w_out_even"], i,
                             prm["ln1_g"][layer], prm["ln1_b"][layer], tm)
        else:
            q, k, v, d4 = _inproj(x2, prm["w_in_odd"], i, tabs, False, tm)
            k3, v3 = k.reshape(b, s, KVW), v.reshape(b, s, KVW)
            a, nk_t, nv_t = _swa_sample(q.reshape(b, s, QW), k3, v3, win_k_t, win_v_t, i, prm["sinks"][i])
            o, st = _hgrn_sample(d4, lbt[i], prm["gnorm_g"][i], state_hgrn, i)
            win_k.append(_kv_untransposed(nk_t))
            win_v.append(_kv_untransposed(nv_t))
            hgrn.append(st)
            x1 = _outproj_ln(a.reshape(m, QW), o, x2, prm["w_out_odd"], i,
                             prm["ln1_g"][layer], prm["ln1_b"][layer], tm)
        hist_rows = jnp.pad(state_ffn[layer], ((0, 0), (s - (FFN_CONV_WIDTH - 1), 0), (0, 0))).reshape(m, D_FF)
        xo, gp = _ffn_sample(x1, hist_rows, s, prm["w_ffn_gate"], prm["w_ffn_up"],
                             prm["w_ffn_down"], layer, prm["w_ffn_conv"][layer], prm["b_ffn_conv"][layer],
                             prm["ln2_g"][layer], prm["ln2_b"][layer], tm=tm)
        ffn.append(gp.reshape(b, s, D_FF)[:, s - (FFN_CONV_WIDTH - 1):])
        x = xo.reshape(b, s, D_MODEL)
    return (x, jnp.stack(kv_k), jnp.stack(kv_v), jnp.stack(conv_b), jnp.stack(win_k), jnp.stack(win_v),
            jnp.stack(hgrn), jnp.stack(ffn))


def kernel(x_prompt, x_sample, cache_k, cache_v, state_conv_b, state_win_k, state_win_v, state_hgrn, state_ffn,
           page_table, w_in_even, w_dw_b, b_dw_b, conv_ln_g, conv_ln_b, w_out_even, w_in_odd, sinks, lb_param,
           gnorm_g, w_out_odd, ln1_g, ln1_b, ln2_g, ln2_b, w_ffn_gate, w_ffn_up, w_ffn_conv, b_ffn_conv, w_ffn_down):
    prm = dict(w_in_even=w_in_even.astype(BF16), w_dw_b=w_dw_b, b_dw_b=b_dw_b, conv_ln_g=conv_ln_g,
               conv_ln_b=conv_ln_b, w_out_even=w_out_even.astype(BF16), w_in_odd=w_in_odd.astype(BF16), sinks=sinks,
               gnorm_g=gnorm_g, w_out_odd=w_out_odd.astype(BF16), ln1_g=ln1_g, ln1_b=ln1_b, ln2_g=ln2_g, ln2_b=ln2_b,
               w_ffn_gate=w_ffn_gate.astype(BF16), w_ffn_up=w_ffn_up.astype(BF16), w_ffn_conv=w_ffn_conv,
               b_ffn_conv=b_ffn_conv, w_ffn_down=w_ffn_down.astype(BF16))
    lbt = _lb_tables(lb_param)
    yp, kkp, kvp, cbp, wkp, wvp, hgp, ffp = _trunk_prompt(x_prompt, prm, lbt)
    ys, kks, kvs, cbs, wks, wvs, hgs, ffs = _trunk_sample(
        x_sample, prm, lbt, cache_k, cache_v, page_table, state_conv_b, state_win_k, state_win_v, state_hgrn, state_ffn)
    return (yp, ys, kkp, kvp, kks, kvs, cbp, cbs, wkp, wvp, wks, wvs, hgp, hgs, ffp, ffs)
```
